```python
import jax, jax.numpy as jnp
from jax import lax
import numpy as np

D_MODEL = 1024
BATCH = 16
SEQ = 2048
DEPTH = 4
DEC_BATCH = 128
DEC_SEQ = 1
PAST_LEN = 8192
PAGE_SIZE = 128

N_META = 16
N_A_LAYERS = DEPTH // 2
N_B_LAYERS = DEPTH - N_A_LAYERS
HG_HEADS = 8
HG_KDIM = 128
HG_VDIM = D_MODEL // HG_HEADS
HG_FDIM = HG_HEADS * HG_KDIM
HG_CHUNK = 64
LB_FLOOR = 1e-30
MLA_HEADS = 8
KV_LORA = 256
Q_LORA = 384
NOPE_DIM = 128
ROPE_DIM = 64
V_DIM = 128
ROPE_THETA = 10000.0
ATTN_SCALE = (NOPE_DIM + ROPE_DIM) ** -0.5
Q_BLOCK = 128
MASK_VALUE = -1e30
D_FF = 4 * D_MODEL
EPS = 1e-6

kernel_name = "yoco_hgrn2_mla_decoder_step"


def rmsnorm(x, g):
    xf = x.astype(jnp.float32)
    y = xf * lax.rsqrt(jnp.mean(xf * xf, axis=-1, keepdims=True) + EPS)
    return (y * g.astype(jnp.float32)).astype(x.dtype)


def sq_relu_mlp(x, w_up, w_down):
    h = jax.nn.relu(x @ w_up)
    return (h * h) @ w_down


def rope(x, pos):
    half = ROPE_DIM // 2
    inv_freq = ROPE_THETA ** (-jnp.arange(half, dtype=jnp.float32) / half)
    ang = pos.astype(jnp.float32)[:, None] * inv_freq[None, :]
    shape = (ang.shape[0],) + (1,) * (x.ndim - 3) + (half,)
    cos = jnp.cos(ang).reshape(shape)
    sin = jnp.sin(ang).reshape(shape)
    xf = x.astype(jnp.float32)
    x1, x2 = xf[..., :half], xf[..., half:]
    return jnp.concatenate([x1 * cos - x2 * sin, x2 * cos + x1 * sin], axis=-1).astype(x.dtype)


def hgrn2_chunk(S0, q, logf, k, v):
    L = q.shape[1]
    b = jnp.cumsum(logf, axis=1)
    causal = jnp.tril(jnp.ones((L, L), dtype=bool))[None, :, :, None, None]
    diff = b[:, :, None] - b[:, None, :]
    decay = jnp.where(causal, jnp.exp(jnp.where(causal, diff, 0.0)), 0.0)
    scores = jnp.einsum('bthk,btshk,bshk->bhts', q, decay, k)
    out = (jnp.einsum('bthk,bhkv->bthv', q * jnp.exp(b), S0)
           + jnp.einsum('bhts,bshv->bthv', scores, v))
    b_last = b[:, -1]
    S = (jnp.exp(b_last)[..., None] * S0
         + jnp.einsum('bshk,bshv->bhkv', k * jnp.exp(b_last[:, None] - b), v))
    return S, out


def hgrn2_recurrence(S0, q, logf, k, v, n_lead):
    Bsz, L = q.shape[0], q.shape[1]
    S, out_lead = hgrn2_chunk(S0, q[:, :n_lead], logf[:, :n_lead], k[:, :n_lead], v[:, :n_lead])
    rest = L - n_lead
    if rest == 0:
        return S, out_lead
    n_chunks = rest // HG_CHUNK

    def to_chunks(t):
        t = t[:, n_lead:]
        return jnp.moveaxis(t.reshape((Bsz, n_chunks, HG_CHUNK) + t.shape[2:]), 1, 0)

    S, out_rest = lax.scan(lambda s, c: hgrn2_chunk(s, c[0], c[1], c[2], c[3]), S,
                           (to_chunks(q), to_chunks(logf), to_chunks(k), to_chunks(v)))
    out_rest = jnp.moveaxis(out_rest, 0, 1).reshape(Bsz, rest, HG_HEADS, HG_VDIM)
    return S, jnp.concatenate([out_lead, out_rest], axis=1)


def hgrn2_mixer(xn, S0, w_q, w_f, w_i, w_g, g_norm, w_o, lb, n_lead):
    Bsz, L, _ = xn.shape
    f32 = jnp.float32
    hk = (Bsz, L, HG_HEADS, HG_KDIM)
    hv = (Bsz, L, HG_HEADS, HG_VDIM)
    q = jax.nn.silu((xn @ w_q).astype(f32)).reshape(hk)
    z = (xn @ w_f).astype(f32).reshape(hk)
    lbh = lb.reshape(HG_HEADS, HG_KDIM)
    logf = jnp.logaddexp(jnp.log1p(-lbh) + jax.nn.log_sigmoid(z),
                         jnp.log(jnp.maximum(lbh, LB_FLOOR)))
    k = (1.0 - lbh) * jax.nn.sigmoid(-z)
    v = (xn @ w_i).astype(f32).reshape(hv)
    S, o = hgrn2_recurrence(S0.astype(f32), q, logf, k, v, n_lead)
    o = rmsnorm(o, g_norm.reshape(HG_HEADS, HG_VDIM))
    o = o * jax.nn.silu((xn @ w_g).astype(f32)).reshape(hv)
    return o.reshape(Bsz, L, D_MODEL).astype(xn.dtype) @ w_o, S.astype(S0.dtype)


def shared_kv(h, pos, norm_kv, w_dkv, kv_norm, w_kr):
    hn = rmsnorm(h, norm_kv)
    ckv = rmsnorm(hn @ w_dkv, kv_norm)
    krope = rope(hn @ w_kr, pos)
    return ckv, krope


def mla_queries(xn, pos, w_dq, q_norm, w_uq, w_uk):
    cq = rmsnorm(xn @ w_dq, q_norm)
    q = jnp.einsum('bsr,rhd->bshd', cq, w_uq)
    q_nope = q[..., :NOPE_DIM]
    q_rope = rope(q[..., NOPE_DIM:], pos)
    q_lat = jnp.einsum('bshn,rhn->bshr', q_nope, w_uk)
    return q_lat, q_rope


def mla_output(o_lat, w_uv, w_o):
    Bsz, L = o_lat.shape[0], o_lat.shape[1]
    o = jnp.einsum('bshr,rhv->bshv', o_lat, w_uv)
    return o.reshape(Bsz, L, MLA_HEADS * V_DIM) @ w_o


def mla_scores(q_lat, q_rope, ckv, krope):
    s = jnp.einsum('bqhr,bkr->bhqk', q_lat, ckv) + jnp.einsum('bqhp,bkp->bhqk', q_rope, krope)
    return s.astype(jnp.float32) * ATTN_SCALE


def attend_dense(q_lat, q_rope, q_pos, ckv, krope, k_pos):
    s = mla_scores(q_lat, q_rope, ckv, krope)
    s = jnp.where((k_pos[None, :] <= q_pos[:, None])[None, None], s, MASK_VALUE)
    p = jax.nn.softmax(s, axis=-1).astype(ckv.dtype)
    return jnp.einsum('bhqk,bkr->bqhr', p, ckv)


def trunk(h, pos, hg_state0, n_lead, attend, p):
    lbs = jax.nn.softmax(p["hg_lower_bounds"].astype(jnp.float32), axis=0)
    lbs = jnp.cumsum(lbs, axis=0) - lbs[0]
    new_states = []
    ckv = krope = None
    for layer in range(DEPTH):
        xn = rmsnorm(h, p["norm_mix"][layer])
        if layer < N_A_LAYERS:
            mix, s_new = hgrn2_mixer(xn, hg_state0[layer], p["hg_wq"][layer], p["hg_wf"][layer],
                                     p["hg_wi"][layer], p["hg_wg"][layer], p["hg_gnorm"][layer],
                                     p["hg_wo"][layer], lbs[layer], n_lead)
            new_states.append(s_new)
        else:
            j = layer - N_A_LAYERS
            q_lat, q_rope = mla_queries(xn, pos, p["w_dq"][j], p["q_norm"][j], p["w_uq"][j], p["w_uk"])
            mix = mla_output(attend(q_lat, q_rope, pos, ckv, krope), p["w_uv"], p["w_o"][j])
        h = h + mix.astype(h.dtype)
        h = h + sq_relu_mlp(rmsnorm(h, p["norm_mlp"][layer]), p["w_up"][layer], p["w_down"][layer])
        if layer == N_A_LAYERS - 1:
            ckv, krope = shared_kv(h, pos, p["norm_kv"], p["w_dkv"], p["kv_norm"], p["w_kr"])
    return rmsnorm(h, p["norm_final"]), jnp.stack(new_states), ckv, krope


def setup_inputs(seed: int = 0) -> dict:
    key = jax.random.key(seed)
    ks = iter(jax.random.split(key, 40))
    f32 = jnp.float32

    def w(shape, fan_in):
        return jax.random.normal(next(ks), shape, f32) * (fan_in ** -0.5)

    def gain(shape):
        return 1.0 + 0.02 * jax.random.normal(next(ks), shape, f32)

    n_pages = PAST_LEN // PAGE_SIZE
    n_phys = (DEC_BATCH * n_pages * 5) // 4
    x_prompt = jax.random.normal(next(ks), (BATCH, SEQ, D_MODEL), f32)
    x_sample = jax.random.normal(next(ks), (DEC_BATCH, DEC_SEQ, D_MODEL), f32)
    state_hgrn = 0.5 * jax.random.normal(next(ks), (N_A_LAYERS, DEC_BATCH, HG_HEADS, HG_KDIM, HG_VDIM), f32)
    cache_ckv = jax.random.normal(next(ks), (n_phys, PAGE_SIZE, KV_LORA), f32)
    cache_krope = jax.random.normal(next(ks), (n_phys, PAGE_SIZE, ROPE_DIM), f32)
    page_table = jax.random.permutation(next(ks), n_phys)[: DEC_BATCH * n_pages].reshape(
        DEC_BATCH, n_pages).astype(jnp.int32)
    return {
        "x_prompt": x_prompt,
        "x_sample": x_sample,
        "state_hgrn": state_hgrn,
        "cache_ckv": cache_ckv,
        "cache_krope": cache_krope,
        "page_table": page_table,
        "meta_tokens": jax.random.normal(next(ks), (N_META, D_MODEL), f32),
        "norm_mix": gain((DEPTH, D_MODEL)),
        "norm_mlp": gain((DEPTH, D_MODEL)),
        "norm_final": gain((D_MODEL,)),
        "hg_wq": w((N_A_LAYERS, D_MODEL, HG_FDIM), D_MODEL),
        "hg_wf": w((N_A_LAYERS, D_MODEL, HG_FDIM), D_MODEL),
        "hg_wi": w((N_A_LAYERS, D_MODEL, D_MODEL), D_MODEL),
        "hg_wg": w((N_A_LAYERS, D_MODEL, D_MODEL), D_MODEL),
        "hg_gnorm": gain((N_A_LAYERS, D_MODEL)),
        "hg_wo": w((N_A_LAYERS, D_MODEL, D_MODEL), D_MODEL),
        "hg_lower_bounds": jax.random.normal(next(ks), (N_A_LAYERS, HG_FDIM), f32),
        "norm_kv": gain((D_MODEL,)),
        "w_dkv": w((D_MODEL, KV_LORA), D_MODEL),
        "kv_norm": gain((KV_LORA,)),
        "w_kr": w((D_MODEL, ROPE_DIM), D_MODEL),
        "w_uk": w((KV_LORA, MLA_HEADS, NOPE_DIM), KV_LORA),
        "w_uv": w((KV_LORA, MLA_HEADS, V_DIM), KV_LORA),
        "w_dq": w((N_B_LAYERS, D_MODEL, Q_LORA), D_MODEL),
        "q_norm": gain((N_B_LAYERS, Q_LORA)),
        "w_uq": w((N_B_LAYERS, Q_LORA, MLA_HEADS, NOPE_DIM + ROPE_DIM), Q_LORA),
        "w_o": w((N_B_LAYERS, MLA_HEADS * V_DIM, D_MODEL), MLA_HEADS * V_DIM),
        "w_up": w((DEPTH, D_MODEL, D_FF), D_MODEL),
        "w_down": w((DEPTH, D_FF, D_MODEL), D_FF),
    }


def reference(x_prompt, x_sample, state_hgrn, cache_ckv, cache_krope, page_table,
              meta_tokens, norm_mix, norm_mlp, norm_final,
              hg_wq, hg_wf, hg_wi, hg_wg, hg_gnorm, hg_wo, hg_lower_bounds,
              norm_kv, w_dkv, kv_norm, w_kr, w_uk, w_uv,
              w_dq, q_norm, w_uq, w_o, w_up, w_down):
    p = dict(norm_mix=norm_mix, norm_mlp=norm_mlp, norm_final=norm_final,
             hg_wq=hg_wq, hg_wf=hg_wf, hg_wi=hg_wi, hg_wg=hg_wg, hg_gnorm=hg_gnorm, hg_wo=hg_wo,
             hg_lower_bounds=hg_lower_bounds, norm_kv=norm_kv, w_dkv=w_dkv, kv_norm=kv_norm,
             w_kr=w_kr, w_uk=w_uk, w_uv=w_uv, w_dq=w_dq, q_norm=q_norm, w_uq=w_uq, w_o=w_o,
             w_up=w_up, w_down=w_down)

    b_p, seq = x_prompt.shape[0], x_prompt.shape[1]
    meta = jnp.broadcast_to(meta_tokens.astype(x_prompt.dtype)[None], (b_p, N_META, D_MODEL))
    h_p = jnp.concatenate([meta, x_prompt], axis=1)
    pos_p = jnp.arange(N_META + seq, dtype=jnp.int32)
    s0_p = jnp.zeros((N_A_LAYERS, b_p, HG_HEADS, HG_KDIM, HG_VDIM), jnp.float32)
    n_blk = seq // Q_BLOCK

    def prompt_attend(q_lat, q_rope, pos, ckv, krope):
        o_meta = attend_dense(q_lat[:, :N_META], q_rope[:, :N_META], pos[:N_META],
                              ckv[:, :N_META], krope[:, :N_META], pos[:N_META])

        def blocks(t):
            t = t[:, N_META:]
            return jnp.moveaxis(t.reshape((b_p, n_blk, Q_BLOCK) + t.shape[2:]), 1, 0)

        o = lax.map(lambda a: attend_dense(a[0], a[1], a[2], ckv, krope, pos),
                    (blocks(q_lat), blocks(q_rope), pos[N_META:].reshape(n_blk, Q_BLOCK)))
        o = jnp.moveaxis(o, 0, 1).reshape(b_p, seq, MLA_HEADS, KV_LORA)
        return jnp.concatenate([o_meta, o], axis=1)

    out_p, state_hgrn_prompt, ckv_prompt, krope_prompt = trunk(h_p, pos_p, s0_p, N_META, prompt_attend, p)
    y_prompt = out_p[:, N_META:]

    b_s, dec = x_sample.shape[0], x_sample.shape[1]
    pos_s = PAST_LEN + jnp.arange(dec, dtype=jnp.int32)
    ckv_past = cache_ckv[page_table].reshape(b_s, -1, KV_LORA)
    krope_past = cache_krope[page_table].reshape(b_s, -1, ROPE_DIM)
    n_past = ckv_past.shape[1]

    def sample_attend(q_lat, q_rope, pos, ckv_new, krope_new):
        s_past = mla_scores(q_lat, q_rope, ckv_past, krope_past)
        s_new = mla_scores(q_lat, q_rope, ckv_new, krope_new)
        s_new = jnp.where((pos[None, :] <= pos[:, None])[None, None], s_new, MASK_VALUE)
        pr = jax.nn.softmax(jnp.concatenate([s_past, s_new], axis=-1), axis=-1).astype(ckv_new.dtype)
        return (jnp.einsum('bhqk,bkr->bqhr', pr[..., :n_past], ckv_past)
                + jnp.einsum('bhqk,bkr->bqhr', pr[..., n_past:], ckv_new))

    y_sample, state_hgrn_sample, ckv_sample, krope_sample = trunk(x_sample, pos_s, state_hgrn, dec,
                                                                  sample_attend, p)
    return (y_prompt, y_sample, state_hgrn_prompt, ckv_prompt, krope_prompt,
            state_hgrn_sample, ckv_sample, krope_sample)
```

```python
import functools

import jax
import jax.numpy as jnp
from jax import lax
from jax.experimental import pallas as pl
from jax.experimental.pallas import tpu as pltpu

F32 = jnp.float32
BF16 = jnp.bfloat16

EPS = 1e-6
LB_FLOOR = 1e-30
ROPE_THETA = 10000.0
MASK_VALUE = -1e30

HG_HEADS = 8
MLA_HEADS = 8
HG_CHUNK = 64
HG_SUB = 16
ATTN_TILE = 256
ROPE_PAD = 128
PAGES_PER_STEP = 16
DEC_STATE_BLOCK = 4
VMEM_LIMIT_BYTES = 56 * 1024 * 1024

NT = (((1,), (1,)), ((), ()))
TN = (((0,), (0,)), ((), ()))


def _params(*sem):
    return pltpu.CompilerParams(dimension_semantics=sem, vmem_limit_bytes=VMEM_LIMIT_BYTES)


def _rms(x, g):
    return x * lax.rsqrt(jnp.mean(x * x, axis=-1, keepdims=True) + EPS) * g


def _sigmoid(x):
    return 1.0 / (1.0 + jnp.exp(-x))


def _dot(a, b):
    return jnp.dot(a, b, preferred_element_type=F32)


def _row_tile(seq_len, cap=1024):
    best = None
    for t in range(16, min(seq_len, cap) + 1, 16):
        if seq_len % t == 0:
            best = t
    assert best is not None, seq_len
    return best


def _hg_proj_kernel(h_ref, g_ref, wq_ref, wf_ref, wi_ref, wg_ref, lb_ref,
                    q_ref, lf_ref, k_ref, v_ref, gate_ref):
    xn = _rms(h_ref[...], g_ref[...]).astype(BF16)
    q = _dot(xn, wq_ref[...])
    q_ref[...] = (q * _sigmoid(q)).astype(q_ref.dtype)
    z = _dot(xn, wf_ref[...])
    lb = lb_ref[...]
    log_1m_lb = jnp.log1p(-lb)
    log_lb = jnp.log(jnp.maximum(lb, LB_FLOOR))
    t = jnp.exp(-jnp.abs(z))
    a = log_1m_lb + (jnp.minimum(z, 0.0) - jnp.log1p(t))
    lf_ref[...] = jnp.maximum(a, log_lb) + jnp.log1p(jnp.exp(-jnp.abs(a - log_lb)))
    k_ref[...] = ((1.0 - lb) * (jnp.where(z >= 0.0, t, 1.0) / (1.0 + t))).astype(k_ref.dtype)
    v_ref[...] = _dot(xn, wi_ref[...]).astype(v_ref.dtype)
    gt = _dot(xn, wg_ref[...])
    gate_ref[...] = (gt * _sigmoid(gt)).astype(gate_ref.dtype)


def _hg_proj(h, g, wq, wf, wi, wg, lb, tm, act_dtype):
    T, D = h.shape
    N = wq.shape[1]
    row = pl.BlockSpec((tm, D), lambda i: (i, 0))
    vec = lambda n: pl.BlockSpec((1, n), lambda i: (0, 0))
    mat = lambda a: pl.BlockSpec(a.shape, lambda i: (0, 0))
    out = lambda n: pl.BlockSpec((tm, n), lambda i: (i, 0))
    return pl.pallas_call(
        _hg_proj_kernel,
        grid=(T // tm,),
        in_specs=[row, vec(D), mat(wq), mat(wf), mat(wi), mat(wg), vec(N)],
        out_specs=[out(N), out(N), out(N), out(D), out(D)],
        out_shape=[jax.ShapeDtypeStruct((T, N), act_dtype), jax.ShapeDtypeStruct((T, N), F32),
                   jax.ShapeDtypeStruct((T, N), act_dtype), jax.ShapeDtypeStruct((T, D), act_dtype),
                   jax.ShapeDtypeStruct((T, D), act_dtype)],
        compiler_params=_params("parallel"),
        name="hg_proj",
    )(h, g, wq, wf, wi, wg, lb)


def _hg_rec_kernel(q_ref, lf_ref, k_ref, v_ref, gate_ref, gn_ref, o_ref, s_ref, st_scr,
                   *, seq_len):
    st_scr[...] = jnp.zeros_like(st_scr)
    kdim = q_ref.shape[-1]
    ones_kk = jnp.ones((kdim, kdim), BF16)
    gn = gn_ref[...]

    def chunk(r0, C):
        rows = pl.ds(r0, C)
        lf = lf_ref[0, rows, :]
        tri = (lax.broadcasted_iota(jnp.int32, (C, C), 0)
               >= lax.broadcasted_iota(jnp.int32, (C, C), 1)).astype(BF16)
        hi = lf.astype(BF16)
        r1 = lf - hi.astype(F32)
        mid = r1.astype(BF16)
        lo = (r1 - mid.astype(F32)).astype(BF16)
        b = _dot(tri, hi) + _dot(tri, mid) + _dot(tri, lo)
        q = q_ref[0, rows, :].astype(F32)
        k = k_ref[0, rows, :].astype(F32)
        vb = v_ref[0, rows, :].astype(BF16)
        vf = vb.astype(F32)
        st = st_scr[...]
        o = lax.dot_general((q * jnp.exp(b)).astype(BF16), st.astype(BF16), NT,
                            preferred_element_type=F32)
        parts = []
        t_idx = lax.broadcasted_iota(jnp.int32, (HG_SUB, 1), 0)
        for i in range(C // HG_SUB):
            lo_r, hi_r = i * HG_SUB, (i + 1) * HG_SUB
            b_i, q_i, k_i, v_i = b[lo_r:hi_r], q[lo_r:hi_r], k[lo_r:hi_r], vf[lo_r:hi_r]
            o_i = jnp.zeros((HG_SUB, vf.shape[1]), F32)
            if i > 0:
                anchor = b[lo_r - 1:lo_r]
                qd = (q_i * jnp.exp(b_i - anchor)).astype(BF16)
                kd = (k[:lo_r] * jnp.exp(anchor - b[:lo_r])).astype(BF16)
                sc = lax.dot_general(qd, kd, NT, preferred_element_type=F32)
                o_i = _dot(sc.astype(BF16), vb[:lo_r])
            prods = []
            for s in range(HG_SUB):
                e = jnp.where(t_idx >= s, jnp.exp(b_i - b_i[s:s + 1]), 0.0)
                prods.append((q_i * e * k_i[s:s + 1]).astype(BF16))
            rs = _dot(jnp.concatenate(prods, axis=0), ones_kk)
            for s in range(HG_SUB):
                o_i = o_i + rs[s * HG_SUB:(s + 1) * HG_SUB] * v_i[s:s + 1]
            parts.append(o_i)
        o = o + (jnp.concatenate(parts, axis=0) if len(parts) > 1 else parts[0])
        b_last = b[C - 1:C]
        kd = (k * jnp.exp(b_last - b)).astype(BF16)
        st_scr[...] = st * jnp.exp(b_last) + lax.dot_general(vb, kd, TN,
                                                             preferred_element_type=F32)
        gate = gate_ref[0, rows, :].astype(F32)
        o_ref[0, rows, :] = (_rms(o, gn) * gate).astype(o_ref.dtype)

    n_full = seq_len // HG_CHUNK
    tail = seq_len - n_full * HG_CHUNK

    def body(c, carry):
        chunk(pl.multiple_of(c * HG_CHUNK, HG_CHUNK), HG_CHUNK)
        return carry

    lax.fori_loop(0, n_full, body, 0)
    if tail:
        chunk(n_full * HG_CHUNK, tail)
    s_ref[0, 0] = st_scr[...].T


def _hg_rec(q, lf, k, v, gate, gn, batch, seq_len):
    T, N = q.shape
    D = v.shape[1]
    kd, vd = N // HG_HEADS, D // HG_HEADS
    assert seq_len % HG_SUB == 0
    r3 = lambda a: a.reshape(batch, seq_len, a.shape[1])
    seq_k = pl.BlockSpec((1, seq_len, kd), lambda b, h: (b, 0, h))
    seq_v = pl.BlockSpec((1, seq_len, vd), lambda b, h: (b, 0, h))
    o, s = pl.pallas_call(
        functools.partial(_hg_rec_kernel, seq_len=seq_len),
        grid=(batch, HG_HEADS),
        in_specs=[seq_k, seq_k, seq_k, seq_v, seq_v, pl.BlockSpec((1, vd), lambda b, h: (0, h))],
        out_specs=[seq_v, pl.BlockSpec((1, 1, kd, vd), lambda b, h: (b, h, 0, 0))],
        out_shape=[jax.ShapeDtypeStruct((batch, seq_len, D), BF16),
                   jax.ShapeDtypeStruct((batch, HG_HEADS, kd, vd), F32)],
        scratch_shapes=[pltpu.VMEM((vd, kd), F32)],
        compiler_params=_params("parallel", "parallel"),
        name="hg_rec",
    )(r3(q), r3(lf), r3(k), r3(v), r3(gate), gn)
    return o.reshape(T, D), s


def _hg_dec_kernel(q_ref, lf_ref, k_ref, v_ref, gate_ref, gn_ref, s0_ref, o_ref, s_ref):
    nb, nh, _, vd = s0_ref.shape
    for b in range(nb):
        outs = []
        for h in range(nh):
            f_col = jnp.exp(lf_ref[b, :, h:h + 1])
            k_col = k_ref[b, :, h:h + 1]
            q_col = q_ref[b, :, h:h + 1]
            v_row = v_ref[b, :, h * vd:(h + 1) * vd]
            s_new = f_col * s0_ref[b, h] + k_col * v_row
            s_ref[b, h] = s_new
            o = jnp.sum(s_new * q_col, axis=0, keepdims=True)
            outs.append(_rms(o, gn_ref[:, h * vd:(h + 1) * vd]))
        o_all = jnp.concatenate(outs, axis=1) * gate_ref[b]
        o_ref[b] = o_all.astype(o_ref.dtype)


def _hg_dec(q, lf, k, v, gate, gn, s0):
    nseq, nh, kd, vd = s0.shape
    D = nh * vd
    nb = DEC_STATE_BLOCK if nseq % DEC_STATE_BLOCK == 0 else 1
    km = lambda a: jnp.transpose(a.reshape(nseq, nh, kd), (0, 2, 1))
    col = pl.BlockSpec((nb, kd, nh), lambda i: (i, 0, 0))
    rowv = pl.BlockSpec((nb, 1, D), lambda i: (i, 0, 0))
    st = pl.BlockSpec((nb, nh, kd, vd), lambda i: (i, 0, 0, 0))
    o, s = pl.pallas_call(
        _hg_dec_kernel,
        grid=(nseq // nb,),
        in_specs=[col, col, col, rowv, rowv, pl.BlockSpec((1, D), lambda i: (0, 0)), st],
        out_specs=[rowv, st],
        out_shape=[jax.ShapeDtypeStruct((nseq, 1, D), BF16), jax.ShapeDtypeStruct(s0.shape, F32)],
        compiler_params=_params("parallel"),
        name="hg_dec",
    )(km(q), km(lf), km(k), v.reshape(nseq, 1, D), gate.reshape(nseq, 1, D), gn, s0)
    return o.reshape(nseq, D), s


def _mlp_kernel(h_ref, mix_ref, wmix_ref, g_ref, wup_ref, wdn_ref, gf_ref, o_ref, xn_scr,
                *, final_norm):
    j = pl.program_id(1)

    @pl.when(j == 0)
    def _():
        h1 = h_ref[...] + _dot(mix_ref[...], wmix_ref[...])
        o_ref[...] = h1
        xn_scr[...] = _rms(h1, g_ref[...]).astype(BF16)

    u = jnp.maximum(_dot(xn_scr[...], wup_ref[...]), 0.0)
    o_ref[...] += _dot((u * u).astype(BF16), wdn_ref[...])

    if final_norm:
        @pl.when(j == pl.num_programs(1) - 1)
        def _():
            o_ref[...] = _rms(o_ref[...], gf_ref[...])


def _mlp(h, mix, wmix, g, wup, wdn, gf, tm, final_norm):
    T, D = h.shape
    dff = wup.shape[1]
    tf = min(dff, 1024)
    row = lambda: pl.BlockSpec((tm, D), lambda i, j: (i, 0))
    vec = lambda: pl.BlockSpec((1, D), lambda i, j: (0, 0))
    return pl.pallas_call(
        functools.partial(_mlp_kernel, final_norm=final_norm),
        grid=(T // tm, dff // tf),
        in_specs=[row(), row(), pl.BlockSpec((D, D), lambda i, j: (0, 0)), vec(),
                  pl.BlockSpec((D, tf), lambda i, j: (0, j)),
                  pl.BlockSpec((tf, D), lambda i, j: (j, 0)), vec()],
        out_specs=row(),
        out_shape=jax.ShapeDtypeStruct((T, D), F32),
        scratch_shapes=[pltpu.VMEM((tm, D), BF16)],
        compiler_params=_params("parallel", "arbitrary"),
        name="mlp",
    )(h, mix, wmix, g, wup, wdn, gf)


def _kv_kernel(h_ref, g_ref, wdkv_ref, gkv_ref, wkr_ref, wkrs_ref, cos_ref, sin_ref,
               ckv_ref, kr_ref, kvb_ref):
    hn = _rms(h_ref[...], g_ref[...]).astype(BF16)
    ckv = _rms(_dot(hn, wdkv_ref[...]), gkv_ref[...])
    kr = _dot(hn, wkr_ref[...]) * cos_ref[...] + _dot(hn, wkrs_ref[...]) * sin_ref[...]
    ckv_ref[...] = ckv
    kr_ref[...] = kr
    r, rd = ckv.shape[1], kr.shape[1]
    kvb_ref[:, :r] = ckv.astype(BF16)
    kvb_ref[:, r:r + rd] = kr.astype(BF16)
    kvb_ref[:, r + rd:] = jnp.zeros((kr.shape[0], kvb_ref.shape[1] - r - rd), BF16)


def _kv(h, g, wdkv, gkv, wkr, wkrs, cos, sin, tm):
    T, D = h.shape
    r, rd = wdkv.shape[1], wkr.shape[1]
    npos = cos.shape[0] // tm
    row = lambda n: pl.BlockSpec((tm, n), lambda i: (i, 0))
    vec = lambda n: pl.BlockSpec((1, n), lambda i: (0, 0))
    mat = lambda a: pl.BlockSpec(a.shape, lambda i: (0, 0))
    tab = pl.BlockSpec((tm, rd), lambda i: (i % npos, 0))
    return pl.pallas_call(
        _kv_kernel,
        grid=(T // tm,),
        in_specs=[row(D), vec(D), mat(wdkv), vec(r), mat(wkr), mat(wkrs), tab, tab],
        out_specs=[row(r), row(rd), row(r + ROPE_PAD)],
        out_shape=[jax.ShapeDtypeStruct((T, r), F32), jax.ShapeDtypeStruct((T, rd), F32),
                   jax.ShapeDtypeStruct((T, r + ROPE_PAD), BF16)],
        compiler_params=_params("parallel"),
        name="shared_kv",
    )(h, g, wdkv, gkv, wkr, wkrs, cos, sin)


def _q_kernel(h_ref, g_ref, wdq_ref, gq_ref, wn_ref, wr_ref, wrs_ref, wuk_ref, cos_ref, sin_ref,
              o_ref, *, scale):
    xn = _rms(h_ref[...], g_ref[...]).astype(BF16)
    cq = _rms(_dot(xn, wdq_ref[...]), gq_ref[...]).astype(BF16)
    qn = (_dot(cq, wn_ref[...]) * scale).astype(BF16)
    qr = ((_dot(cq, wr_ref[...]) * cos_ref[...] + _dot(cq, wrs_ref[...]) * sin_ref[...])
          * scale).astype(BF16)
    nh, nd, r = wuk_ref.shape
    rd = qr.shape[1] // nh
    for h in range(nh):
        o_ref[h, :, :r] = _dot(qn[:, h * nd:(h + 1) * nd], wuk_ref[h]).astype(BF16)
        o_ref[h, :, r:r + rd] = qr[:, h * rd:(h + 1) * rd]
        o_ref[h, :, r + rd:] = jnp.zeros((qr.shape[0], o_ref.shape[2] - r - rd), BF16)


def _mla_q(h, g, wdq, gq, wn, wr, wrs, wuk, cos, sin, tm, scale):
    T, D = h.shape
    nh, _, r = wuk.shape
    npos = cos.shape[0] // tm
    vec = lambda n: pl.BlockSpec((1, n), lambda i: (0, 0))
    mat = lambda a: pl.BlockSpec(a.shape, lambda i: (0,) * a.ndim)
    tab = pl.BlockSpec((tm, cos.shape[1]), lambda i: (i % npos, 0))
    return pl.pallas_call(
        functools.partial(_q_kernel, scale=scale),
        grid=(T // tm,),
        in_specs=[pl.BlockSpec((tm, D), lambda i: (i, 0)), vec(D), mat(wdq), vec(wdq.shape[1]),
                  mat(wn), mat(wr), mat(wrs), mat(wuk), tab, tab],
        out_specs=pl.BlockSpec((nh, tm, r + ROPE_PAD), lambda i: (0, i, 0)),
        out_shape=jax.ShapeDtypeStruct((nh, T, r + ROPE_PAD), BF16),
        compiler_params=_params("parallel"),
        name="mla_q",
    )(h, g, wdq, gq, wn, wr, wrs, wuk, cos, sin)


def _attn_kernel(q_ref, kv_ref, wuv_ref, o_ref, *, seq_len, r):
    i = pl.program_id(1)
    nh = q_ref.shape[0]
    tq = ATTN_TILE
    n_full = seq_len // tq
    tail = seq_len - n_full * tq
    vd = wuv_ref.shape[1] // nh

    def run(rows):
        m_rows = nh * rows
        q = q_ref[:, 0, :rows, :].reshape(m_rows, q_ref.shape[-1])

        def step(carry, kv, mask):
            m, l, acc = carry
            s = lax.dot_general(q, kv, NT, preferred_element_type=F32)
            if mask is not None:
                s = jnp.where(mask, s, MASK_VALUE)
            m_new = jnp.maximum(m, jnp.max(s, axis=-1, keepdims=True))
            alpha = jnp.exp(m - m_new)
            p = jnp.exp(s - m_new)
            l = alpha * l + jnp.sum(p, axis=-1, keepdims=True)
            acc = alpha * acc + _dot(p.astype(BF16), kv[:, :r])
            return m_new, l, acc

        init = (jnp.full((m_rows, 1), MASK_VALUE, F32), jnp.zeros((m_rows, 1), F32),
                jnp.zeros((m_rows, r), F32))
        carry = lax.fori_loop(
            0, i,
            lambda j, c: step(c, kv_ref[0, pl.ds(pl.multiple_of(j * tq, tq), tq), :], None),
            init)
        t_loc = lax.broadcasted_iota(jnp.int32, (m_rows, rows), 0) % rows
        s_loc = lax.broadcasted_iota(jnp.int32, (m_rows, rows), 1)
        kv_diag = kv_ref[0, pl.ds(pl.multiple_of(i * tq, tq), rows), :]
        _, l, acc = step(carry, kv_diag, s_loc <= t_loc)
        o_lat = (acc / l).astype(BF16)
        for h in range(nh):
            o_ref[0, :rows, h * vd:(h + 1) * vd] = _dot(
                o_lat[h * rows:(h + 1) * rows], wuv_ref[:, h * vd:(h + 1) * vd]).astype(o_ref.dtype)

    if n_full:
        pl.when(i < n_full)(lambda: run(tq))
    if tail:
        pl.when(i == n_full)(lambda: run(tail))


def _attn_prompt(qcat, kvb, wuv, batch, seq_len):
    nh, T, w = qcat.shape
    r = w - ROPE_PAD
    D = wuv.shape[1]
    nq = pl.cdiv(seq_len, ATTN_TILE)
    out = pl.pallas_call(
        functools.partial(_attn_kernel, seq_len=seq_len, r=r),
        grid=(batch, nq),
        in_specs=[pl.BlockSpec((nh, 1, ATTN_TILE, w), lambda b, i: (0, b, i, 0)),
                  pl.BlockSpec((1, seq_len, w), lambda b, i: (b, 0, 0)),
                  pl.BlockSpec(wuv.shape, lambda b, i: (0, 0))],
        out_specs=pl.BlockSpec((1, ATTN_TILE, D), lambda b, i: (b, i, 0)),
        out_shape=jax.ShapeDtypeStruct((batch, seq_len, D), BF16),
        compiler_params=_params("parallel", "parallel"),
        name="attn_prompt",
    )(qcat.reshape(nh, batch, seq_len, w), kvb.reshape(batch, seq_len, w), wuv)
    return out.reshape(T, D)


def _attn_dec_kernel(pt_ref, q_ref, kvn_ref, wuv_ref, *rest, n_pg, r, rd):
    ckv_refs = rest[:n_pg]
    kr_refs = rest[n_pg:2 * n_pg]
    o_ref, m_scr, l_scr, acc_scr = rest[2 * n_pg:]
    j = pl.program_id(1)
    q = q_ref[0]
    q_lat, q_rope = q[:, :r], q[:, r:r + rd]

    @pl.when(j == 0)
    def _():
        m_scr[...] = jnp.full_like(m_scr, MASK_VALUE)
        l_scr[...] = jnp.zeros_like(l_scr)
        acc_scr[...] = jnp.zeros_like(acc_scr)

    pages = [c[0].astype(BF16) for c in ckv_refs]
    s = jnp.concatenate(
        [lax.dot_general(q_lat, pages[p], NT, preferred_element_type=F32)
         + lax.dot_general(q_rope, kr_refs[p][0].astype(BF16), NT, preferred_element_type=F32)
         for p in range(n_pg)], axis=1)
    m = m_scr[...]
    m_new = jnp.maximum(m, jnp.max(s, axis=-1, keepdims=True))
    alpha = jnp.exp(m - m_new)
    p_all = jnp.exp(s - m_new)
    l_scr[...] = alpha * l_scr[...] + jnp.sum(p_all, axis=-1, keepdims=True)
    pg = s.shape[1] // n_pg
    pv = _dot(p_all[:, :pg].astype(BF16), pages[0])
    for p in range(1, n_pg):
        pv = pv + _dot(p_all[:, p * pg:(p + 1) * pg].astype(BF16), pages[p])
    acc_scr[...] = alpha * acc_scr[...] + pv
    m_scr[...] = m_new

    @pl.when(j == pl.num_programs(1) - 1)
    def _():
        kvn = kvn_ref[0].astype(F32)
        s_new = jnp.sum(q.astype(F32) * kvn, axis=-1, keepdims=True)
        m_old = m_scr[...]
        m_fin = jnp.maximum(m_old, s_new)
        a_old = jnp.exp(m_old - m_fin)
        p_new = jnp.exp(s_new - m_fin)
        l_fin = a_old * l_scr[...] + p_new
        o_lat = (a_old * acc_scr[...] + p_new * kvn[:, :r]) / l_fin
        full = _dot(o_lat.astype(BF16), wuv_ref[...])
        nh = full.shape[0]
        vd = full.shape[1] // nh
        head_of_lane = lax.broadcasted_iota(jnp.int32, full.shape, 1) // vd
        head_of_row = lax.broadcasted_iota(jnp.int32, full.shape, 0)
        o_ref[0] = jnp.sum(jnp.where(head_of_lane == head_of_row, full, 0.0), axis=0,
                           keepdims=True).astype(o_ref.dtype)


def _attn_decode(qcat, kvb_new, cache_ckv, cache_kr, page_table, wuv):
    nh, nseq, w = qcat.shape
    r = w - ROPE_PAD
    rd = cache_kr.shape[2]
    page = cache_ckv.shape[1]
    n_pages = page_table.shape[1]
    n_pg = PAGES_PER_STEP if n_pages % PAGES_PER_STEP == 0 else 1
    D = wuv.shape[1]
    q = jnp.transpose(qcat, (1, 0, 2))

    def page_spec(width, p):
        return pl.BlockSpec((1, page, width), lambda b, j, pt: (pt[b, j * n_pg + p], 0, 0))

    grid_spec = pltpu.PrefetchScalarGridSpec(
        num_scalar_prefetch=1,
        grid=(nseq, n_pages // n_pg),
        in_specs=([pl.BlockSpec((1, nh, w), lambda b, j, pt: (b, 0, 0)),
                   pl.BlockSpec((1, 1, w), lambda b, j, pt: (b, 0, 0)),
                   pl.BlockSpec(wuv.shape, lambda b, j, pt: (0, 0))]
                  + [page_spec(r, p) for p in range(n_pg)]
                  + [page_spec(rd, p) for p in range(n_pg)]),
        out_specs=pl.BlockSpec((1, 1, D), lambda b, j, pt: (b, 0, 0)),
        scratch_shapes=[pltpu.VMEM((nh, 1), F32), pltpu.VMEM((nh, 1), F32),
                        pltpu.VMEM((nh, r), F32)],
    )
    out = pl.pallas_call(
        functools.partial(_attn_dec_kernel, n_pg=n_pg, r=r, rd=rd),
        grid_spec=grid_spec,
        out_shape=jax.ShapeDtypeStruct((nseq, 1, D), BF16),
        compiler_params=_params("parallel", "arbitrary"),
        name="attn_decode",
    )(page_table, q, kvb_new.reshape(nseq, 1, w), wuv,
      *([cache_ckv] * n_pg), *([cache_kr] * n_pg))
    return out.reshape(nseq, D)


def _rope_tables(pos, rope_dim):
    half = rope_dim // 2
    inv_freq = ROPE_THETA ** (-jnp.arange(half, dtype=F32) / half)
    ang = pos.astype(F32)[:, None] * inv_freq[None, :]
    cos, sin = jnp.cos(ang), jnp.sin(ang)
    return jnp.concatenate([cos, cos], axis=1), jnp.concatenate([-sin, sin], axis=1)


def _swap_halves(w, rope_dim):
    lead = w.shape[0]
    w3 = w.reshape(lead, -1, rope_dim)
    half = rope_dim // 2
    return jnp.concatenate([w3[..., half:], w3[..., :half]], axis=-1).reshape(w.shape)


def _prep_weights(p):
    bf = lambda a: a.astype(BF16)
    depth = p["norm_mix"].shape[0]
    n_a = p["hg_wq"].shape[0]
    r, nh, nd = p["w_uk"].shape
    rope_dim = p["w_kr"].shape[1]
    lbs = jax.nn.softmax(p["hg_lower_bounds"].astype(F32), axis=0)
    lbs = jnp.cumsum(lbs, axis=0) - lbs[0]
    w_uq = p["w_uq"]
    nb = w_uq.shape[0]
    wn = w_uq[..., :nd].reshape(nb, w_uq.shape[1], nh * nd)
    wr = w_uq[..., nd:].reshape(nb, w_uq.shape[1], nh * rope_dim)
    return dict(
        depth=depth, n_a=n_a, rope_dim=rope_dim, nope_dim=nd,
        norm_mix=p["norm_mix"][:, None, :], norm_mlp=p["norm_mlp"][:, None, :],
        norm_final=p["norm_final"][None, :],
        hg_wq=bf(p["hg_wq"]), hg_wf=bf(p["hg_wf"]), hg_wi=bf(p["hg_wi"]), hg_wg=bf(p["hg_wg"]),
        hg_wo=bf(p["hg_wo"]), hg_gnorm=p["hg_gnorm"][:, None, :], lbs=lbs[:, None, :],
        norm_kv=p["norm_kv"][None, :], w_dkv=bf(p["w_dkv"]), kv_norm=p["kv_norm"][None, :],
        w_kr=bf(p["w_kr"]), w_kr_sw=bf(_swap_halves(p["w_kr"], rope_dim)),
        w_ukt=bf(jnp.transpose(p["w_uk"], (1, 2, 0))),
        w_uv=bf(p["w_uv"].reshape(r, -1)),
        w_dq=bf(p["w_dq"]), q_norm=p["q_norm"][:, None, :],
        w_uq_n=bf(wn), w_uq_r=bf(wr),
        w_uq_rs=bf(jnp.stack([_swap_halves(wr[j], rope_dim) for j in range(nb)])),
        w_o=bf(p["w_o"]), w_up=bf(p["w_up"]), w_down=bf(p["w_down"]),
    )


def _trunk(h, pos_tile, tm, w, hg_rec_fn, attend_fn, act_dtype):
    rope_dim = w["rope_dim"]
    cos, sin = _rope_tables(pos_tile, rope_dim)
    cos_q, sin_q = jnp.tile(cos, (1, MLA_HEADS)), jnp.tile(sin, (1, MLA_HEADS))
    scale = float((w["nope_dim"] + rope_dim) ** -0.5)
    depth, n_a = w["depth"], w["n_a"]
    states = []
    ckv = krope = kvb = None
    for layer in range(depth):
        last = layer == depth - 1
        if layer < n_a:
            q, lf, k, v, gate = _hg_proj(h, w["norm_mix"][layer], w["hg_wq"][layer], w["hg_wf"][layer],
                                         w["hg_wi"][layer], w["hg_wg"][layer], w["lbs"][layer],
                                         tm, act_dtype)
            mix, s_new = hg_rec_fn(layer, q, lf, k, v, gate, w["hg_gnorm"][layer])
            states.append(s_new)
            w_mix = w["hg_wo"][layer]
        else:
            j = layer - n_a
            qcat = _mla_q(h, w["norm_mix"][layer], w["w_dq"][j], w["q_norm"][j], w["w_uq_n"][j],
                          w["w_uq_r"][j], w["w_uq_rs"][j], w["w_ukt"], cos_q, sin_q, tm, scale)
            mix = attend_fn(qcat, kvb)
            w_mix = w["w_o"][j]
        h = _mlp(h, mix, w_mix, w["norm_mlp"][layer], w["w_up"][layer], w["w_down"][layer],
                 w["norm_final"], tm, final_norm=last)
        if layer == n_a - 1:
            ckv, krope, kvb = _kv(h, w["norm_kv"], w["w_dkv"], w["kv_norm"], w["w_kr"], w["w_kr_sw"],
                                  cos, sin, tm)
    return h, jnp.stack(states), ckv, krope


def kernel(x_prompt, x_sample, state_hgrn, cache_ckv, cache_krope, page_table, meta_tokens, norm_mix, norm_mlp, norm_final, hg_wq, hg_wf, hg_wi, hg_wg, hg_gnorm, hg_wo, hg_lower_bounds, norm_kv, w_dkv, kv_norm, w_kr, w_uk, w_uv, w_dq, q_norm, w_uq, w_o, w_up, w_down):
    w = _prep_weights(dict(
        norm_mix=norm_mix, norm_mlp=norm_mlp, norm_final=norm_final, hg_wq=hg_wq, hg_wf=hg_wf,
        hg_wi=hg_wi, hg_wg=hg_wg, hg_gnorm=hg_gnorm, hg_wo=hg_wo, hg_lower_bounds=hg_lower_bounds,
        norm_kv=norm_kv, w_dkv=w_dkv, kv_norm=kv_norm, w_kr=w_kr, w_uk=w_uk, w_uv=w_uv, w_dq=w_dq,
        q_norm=q_norm, w_uq=w_uq, w_o=w_o, w_up=w_up, w_down=w_down))

    b_p, seq, d = x_prompt.shape
    n_meta = meta_tokens.shape[0]
    seq_len = n_meta + seq
    meta = jnp.broadcast_to(meta_tokens.astype(x_prompt.dtype)[None], (b_p, n_meta, d))
    h_p = jnp.concatenate([meta, x_prompt], axis=1).reshape(b_p * seq_len, d)
    tm_p = _row_tile(seq_len)
    out_p, st_p, ckv_p, kr_p = _trunk(
        h_p, jnp.arange(seq_len, dtype=jnp.int32), tm_p, w,
        lambda layer, q, lf, k, v, gate, gn: _hg_rec(q, lf, k, v, gate, gn, b_p, seq_len),
        lambda qcat, kvb: _attn_prompt(qcat, kvb, w["w_uv"], b_p, seq_len),
        BF16)
    y_prompt = out_p.reshape(b_p, seq_len, d)[:, n_meta:]
    ckv_prompt = ckv_p.reshape(b_p, seq_len, -1)
    krope_prompt = kr_p.reshape(b_p, seq_len, -1)

    b_s, dec, _ = x_sample.shape
    assert dec == 1
    past_len = page_table.shape[1] * cache_ckv.shape[1]
    pos_s = jnp.full((b_s,), past_len, dtype=jnp.int32)
    out_s, st_s, ckv_s, kr_s = _trunk(
        x_sample.reshape(b_s, d), pos_s, b_s, w,
        lambda layer, q, lf, k, v, gate, gn: _hg_dec(q, lf, k, v, gate, gn, state_hgrn[layer]),
        lambda qcat, kvb: _attn_decode(qcat, kvb, cache_ckv, cache_krope, page_table, w["w_uv"]),
        F32)
    return (y_prompt, out_s.reshape(b_s, dec, d), st_p, ckv_prompt, krope_prompt,
            st_s.astype(state_hgrn.dtype), ckv_s.reshape(b_s, dec, -1), kr_s.reshape(b_s, dec, -1))
```

```python
import functools

import jax
import jax.numpy as jnp
from jax import lax
from jax.experimental import pallas as pl
from jax.experimental.pallas import tpu as pltpu

F32 = jnp.float32
BF16 = jnp.bfloat16

EPS = 1e-6
LB_FLOOR = 1e-30
ROPE_THETA = 10000.0
MASK_VALUE = -1e30

HG_HEADS = 8
MLA_HEADS = 8
HG_CHUNK = 64
HG_SUB = 16
HG_HEADS_PER_STEP = 4
SUBLANES = 8
LOG2E = 1.4426950408889634
ATTN_TILE = 256
ROPE_PAD = 128
DEC_STATE_BLOCK = 4
VMEM_LIMIT_BYTES = 56 * 1024 * 1024

NT = (((1,), (1,)), ((), ()))
TN = (((0,), (0,)), ((), ()))


def _params(*sem):
    return pltpu.CompilerParams(dimension_semantics=sem, vmem_limit_bytes=VMEM_LIMIT_BYTES)


def _rms(x, g):
    return x * lax.rsqrt(jnp.mean(x * x, axis=-1, keepdims=True) + EPS) * g


def _sigmoid(x):
    return 1.0 / (1.0 + jnp.exp(-x))


def _dot(a, b):
    return jnp.dot(a, b, preferred_element_type=F32)


def _row_tile(seq_len, cap=1024):
    best = None
    for t in range(16, min(seq_len, cap) + 1, 16):
        if seq_len % t == 0:
            best = t
    assert best is not None, seq_len
    return best


def _hg_proj_kernel(h_ref, g_ref, wq_ref, wf_ref, wi_ref, wg_ref, lb_ref,
                    q_ref, lf_ref, k_ref, v_ref, gate_ref):
    xn = _rms(h_ref[...], g_ref[...]).astype(BF16)
    q = _dot(xn, wq_ref[...])
    q_ref[...] = (q * _sigmoid(q)).astype(q_ref.dtype)
    z = _dot(xn, wf_ref[...])
    lb = lb_ref[...]
    log_1m_lb = jnp.log1p(-lb)
    log_lb = jnp.log(jnp.maximum(lb, LB_FLOOR))
    t = jnp.exp(-jnp.abs(z))
    a = log_1m_lb + (jnp.minimum(z, 0.0) - jnp.log1p(t))
    lf_ref[...] = jnp.maximum(a, log_lb) + jnp.log1p(jnp.exp(-jnp.abs(a - log_lb)))
    k_ref[...] = ((1.0 - lb) * (jnp.where(z >= 0.0, t, 1.0) / (1.0 + t))).astype(k_ref.dtype)
    v_ref[...] = _dot(xn, wi_ref[...]).astype(v_ref.dtype)
    gt = _dot(xn, wg_ref[...])
    gate_ref[...] = (gt * _sigmoid(gt)).astype(gate_ref.dtype)


def _hg_proj(h, g, wq, wf, wi, wg, lb, tm, act_dtype):
    T, D = h.shape
    N = wq.shape[1]
    row = pl.BlockSpec((tm, D), lambda i: (i, 0))
    vec = lambda n: pl.BlockSpec((1, n), lambda i: (0, 0))
    mat = lambda a: pl.BlockSpec(a.shape, lambda i: (0, 0))
    out = lambda n: pl.BlockSpec((tm, n), lambda i: (i, 0))
    return pl.pallas_call(
        _hg_proj_kernel,
        grid=(T // tm,),
        in_specs=[row, vec(D), mat(wq), mat(wf), mat(wi), mat(wg), vec(N)],
        out_specs=[out(N), out(N), out(N), out(D), out(D)],
        out_shape=[jax.ShapeDtypeStruct((T, N), act_dtype), jax.ShapeDtypeStruct((T, N), F32),
                   jax.ShapeDtypeStruct((T, N), act_dtype), jax.ShapeDtypeStruct((T, D), act_dtype),
                   jax.ShapeDtypeStruct((T, D), act_dtype)],
        compiler_params=_params("parallel"),
        name="hg_proj",
    )(h, g, wq, wf, wi, wg, lb)


def _hg_rec_kernel(q_ref, lf_ref, k_ref, v_ref, gate_ref, gn_ref, o_ref, s_ref, st_scr,
                   *, seq_len, heads):
    st_scr[...] = jnp.zeros_like(st_scr)
    kd = q_ref.shape[-1] // heads
    vd = v_ref.shape[-1] // heads
    ones_kk = jnp.ones((kd, kd), BF16)
    t_idx = lax.broadcasted_iota(jnp.int32, (SUBLANES, 1), 0)

    def head_chunk(g, r0, C):
        rows = pl.ds(r0, C)
        kl = slice(g * kd, (g + 1) * kd)
        vl = slice(g * vd, (g + 1) * vd)
        n_sub = C // HG_SUB
        lf = lf_ref[0, rows, kl] * LOG2E
        tri = (lax.broadcasted_iota(jnp.int32, (C, C), 0)
               >= lax.broadcasted_iota(jnp.int32, (C, C), 1)).astype(BF16)
        hi = lf.astype(BF16)
        r1 = lf - hi.astype(F32)
        mid = r1.astype(BF16)
        lo = (r1 - mid.astype(F32)).astype(BF16)
        b = _dot(tri, hi) + _dot(tri, mid) + _dot(tri, lo)
        q = q_ref[0, rows, kl].astype(F32)
        k = k_ref[0, rows, kl].astype(F32)
        vb = v_ref[0, rows, vl].astype(BF16)
        vf = vb.astype(F32)
        yield
        st = st_scr[g]
        o = lax.dot_general((q * jnp.exp2(b)).astype(BF16), st.astype(BF16), NT,
                            preferred_element_type=F32)
        scores = []
        for i in range(1, n_sub):
            lo_r = i * HG_SUB
            anchor = b[lo_r - 1:lo_r]
            qd = (q[lo_r:lo_r + HG_SUB] * jnp.exp2(b[lo_r:lo_r + HG_SUB] - anchor)).astype(BF16)
            kdec = (k[:lo_r] * jnp.exp2(anchor - b[:lo_r])).astype(BF16)
            scores.append(lax.dot_general(qd, kdec, NT, preferred_element_type=F32))
        b_last = b[C - 1:C]
        kdec = (k * jnp.exp2(b_last - b)).astype(BF16)
        st_scr[g] = st * jnp.exp2(b_last) + lax.dot_general(vb, kdec, TN,
                                                            preferred_element_type=F32)
        yield
        off = [None] + [_dot(sc.astype(BF16), vb[:(i + 1) * HG_SUB])
                        for i, sc in enumerate(scores)]
        sums = []
        for i in range(n_sub):
            lo_r, hi_r = i * HG_SUB, (i + 1) * HG_SUB
            b_i, q_i, k_i = b[lo_r:hi_r], q[lo_r:hi_r], k[lo_r:hi_r]
            prods = []
            for s in range(HG_SUB):
                for p0 in range((s // SUBLANES) * SUBLANES, HG_SUB, SUBLANES):
                    e = jnp.exp2(b_i[p0:p0 + SUBLANES] - b_i[s:s + 1])
                    if p0 <= s:
                        e = jnp.where(t_idx >= s - p0, e, 0.0)
                    prods.append(q_i[p0:p0 + SUBLANES] * e * k_i[s:s + 1])
            sums.append(_dot(jnp.concatenate(prods, axis=0).astype(BF16), ones_kk))
            yield
        parts = []
        for i in range(n_sub):
            lo_r = i * HG_SUB
            v_i = vf[lo_r:lo_r + HG_SUB]
            acc = {p0: (jnp.zeros((SUBLANES, vd), F32) if off[i] is None
                        else off[i][p0:p0 + SUBLANES]) for p0 in range(0, HG_SUB, SUBLANES)}
            n = 0
            for s in range(HG_SUB):
                for p0 in range((s // SUBLANES) * SUBLANES, HG_SUB, SUBLANES):
                    acc[p0] = acc[p0] + sums[i][n * SUBLANES:(n + 1) * SUBLANES] * v_i[s:s + 1]
                    n += 1
            parts.extend(acc[p0] for p0 in sorted(acc))
        o = o + jnp.concatenate(parts, axis=0)
        gate = gate_ref[0, rows, vl].astype(F32)
        o_ref[0, rows, vl] = (_rms(o, gn_ref[:, vl]) * gate).astype(o_ref.dtype)

    def chunk(r0, C):
        live = [head_chunk(g, r0, C) for g in range(heads)]
        while live:
            live = [gen for gen in live if next(gen, True) is None]

    n_full = seq_len // HG_CHUNK
    tail = seq_len - n_full * HG_CHUNK

    def body(c, carry):
        chunk(pl.multiple_of(c * HG_CHUNK, HG_CHUNK), HG_CHUNK)
        return carry

    lax.fori_loop(0, n_full, body, 0)
    if tail:
        chunk(n_full * HG_CHUNK, tail)
    for g in range(heads):
        s_ref[0, g] = st_scr[g].T


def _hg_rec(q, lf, k, v, gate, gn, batch, seq_len):
    T, N = q.shape
    D = v.shape[1]
    kd, vd = N // HG_HEADS, D // HG_HEADS
    G = HG_HEADS_PER_STEP
    assert seq_len % HG_SUB == 0 and HG_HEADS % G == 0
    r3 = lambda a: a.reshape(batch, seq_len, a.shape[1])
    seq_k = pl.BlockSpec((1, seq_len, G * kd), lambda b, h: (b, 0, h))
    seq_v = pl.BlockSpec((1, seq_len, G * vd), lambda b, h: (b, 0, h))
    o, s = pl.pallas_call(
        functools.partial(_hg_rec_kernel, seq_len=seq_len, heads=G),
        grid=(batch, HG_HEADS // G),
        in_specs=[seq_k, seq_k, seq_k, seq_v, seq_v,
                  pl.BlockSpec((1, G * vd), lambda b, h: (0, h))],
        out_specs=[seq_v, pl.BlockSpec((1, G, kd, vd), lambda b, h: (b, h, 0, 0))],
        out_shape=[jax.ShapeDtypeStruct((batch, seq_len, D), BF16),
                   jax.ShapeDtypeStruct((batch, HG_HEADS, kd, vd), F32)],
        scratch_shapes=[pltpu.VMEM((G, vd, kd), F32)],
        compiler_params=_params("parallel", "parallel"),
        name="hg_rec",
    )(r3(q), r3(lf), r3(k), r3(v), r3(gate), gn)
    return o.reshape(T, D), s


def _hg_dec_kernel(q_ref, lf_ref, k_ref, v_ref, gate_ref, gn_ref, s0_ref, o_ref, s_ref):
    nb, nh, _, vd = s0_ref.shape
    for b in range(nb):
        outs = []
        for h in range(nh):
            f_col = jnp.exp(lf_ref[b, :, h:h + 1])
            k_col = k_ref[b, :, h:h + 1]
            q_col = q_ref[b, :, h:h + 1]
            v_row = v_ref[b, :, h * vd:(h + 1) * vd]
            s_new = f_col * s0_ref[b, h] + k_col * v_row
            s_ref[b, h] = s_new
            o = jnp.sum(s_new * q_col, axis=0, keepdims=True)
            outs.append(_rms(o, gn_ref[:, h * vd:(h + 1) * vd]))
        o_all = jnp.concatenate(outs, axis=1) * gate_ref[b]
        o_ref[b] = o_all.astype(o_ref.dtype)


def _hg_dec(q, lf, k, v, gate, gn, s0):
    nseq, nh, kd, vd = s0.shape
    D = nh * vd
    nb = DEC_STATE_BLOCK if nseq % DEC_STATE_BLOCK == 0 else 1
    km = lambda a: jnp.transpose(a.reshape(nseq, nh, kd), (0, 2, 1))
    col = pl.BlockSpec((nb, kd, nh), lambda i: (i, 0, 0))
    rowv = pl.BlockSpec((nb, 1, D), lambda i: (i, 0, 0))
    st = pl.BlockSpec((nb, nh, kd, vd), lambda i: (i, 0, 0, 0))
    o, s = pl.pallas_call(
        _hg_dec_kernel,
        grid=(nseq // nb,),
        in_specs=[col, col, col, rowv, rowv, pl.BlockSpec((1, D), lambda i: (0, 0)), st],
        out_specs=[rowv, st],
        out_shape=[jax.ShapeDtypeStruct((nseq, 1, D), BF16), jax.ShapeDtypeStruct(s0.shape, F32)],
        compiler_params=_params("parallel"),
        name="hg_dec",
    )(km(q), km(lf), km(k), v.reshape(nseq, 1, D), gate.reshape(nseq, 1, D), gn, s0)
    return o.reshape(nseq, D), s


def _mlp_kernel(h_ref, mix_ref, wmix_ref, g_ref, wup_ref, wdn_ref, gf_ref, o_ref, xn_scr,
                *, final_norm):
    j = pl.program_id(1)

    @pl.when(j == 0)
    def _():
        h1 = h_ref[...] + _dot(mix_ref[...], wmix_ref[...])
        o_ref[...] = h1
        xn_scr[...] = _rms(h1, g_ref[...]).astype(BF16)

    u = jnp.maximum(_dot(xn_scr[...], wup_ref[...]), 0.0)
    o_ref[...] += _dot((u * u).astype(BF16), wdn_ref[...])

    if final_norm:
        @pl.when(j == pl.num_programs(1) - 1)
        def _():
            o_ref[...] = _rms(o_ref[...], gf_ref[...])


def _mlp(h, mix, wmix, g, wup, wdn, gf, tm, final_norm):
    T, D = h.shape
    dff = wup.shape[1]
    tf = min(dff, 1024)
    row = lambda: pl.BlockSpec((tm, D), lambda i, j: (i, 0))
    vec = lambda: pl.BlockSpec((1, D), lambda i, j: (0, 0))
    return pl.pallas_call(
        functools.partial(_mlp_kernel, final_norm=final_norm),
        grid=(T // tm, dff // tf),
        in_specs=[row(), row(), pl.BlockSpec((D, D), lambda i, j: (0, 0)), vec(),
                  pl.BlockSpec((D, tf), lambda i, j: (0, j)),
                  pl.BlockSpec((tf, D), lambda i, j: (j, 0)), vec()],
        out_specs=row(),
        out_shape=jax.ShapeDtypeStruct((T, D), F32),
        scratch_shapes=[pltpu.VMEM((tm, D), BF16)],
        compiler_params=_params("parallel", "arbitrary"),
        name="mlp",
    )(h, mix, wmix, g, wup, wdn, gf)


def _kv_kernel(h_ref, g_ref, wdkv_ref, gkv_ref, wkr_ref, wkrs_ref, cos_ref, sin_ref,
               ckv_ref, kr_ref, kvb_ref):
    hn = _rms(h_ref[...], g_ref[...]).astype(BF16)
    ckv = _rms(_dot(hn, wdkv_ref[...]), gkv_ref[...])
    kr = _dot(hn, wkr_ref[...]) * cos_ref[...] + _dot(hn, wkrs_ref[...]) * sin_ref[...]
    ckv_ref[...] = ckv
    kr_ref[...] = kr
    r, rd = ckv.shape[1], kr.shape[1]
    kvb_ref[:, :r] = ckv.astype(BF16)
    kvb_ref[:, r:r + rd] = kr.astype(BF16)
    kvb_ref[:, r + rd:] = jnp.zeros((kr.shape[0], kvb_ref.shape[1] - r - rd), BF16)


def _kv(h, g, wdkv, gkv, wkr, wkrs, cos, sin, tm):
    T, D = h.shape
    r, rd = wdkv.shape[1], wkr.shape[1]
    npos = cos.shape[0] // tm
    row = lambda n: pl.BlockSpec((tm, n), lambda i: (i, 0))
    vec = lambda n: pl.BlockSpec((1, n), lambda i: (0, 0))
    mat = lambda a: pl.BlockSpec(a.shape, lambda i: (0, 0))
    tab = pl.BlockSpec((tm, rd), lambda i: (i % npos, 0))
    return pl.pallas_call(
        _kv_kernel,
        grid=(T // tm,),
        in_specs=[row(D), vec(D), mat(wdkv), vec(r), mat(wkr), mat(wkrs), tab, tab],
        out_specs=[row(r), row(rd), row(r + ROPE_PAD)],
        out_shape=[jax.ShapeDtypeStruct((T, r), F32), jax.ShapeDtypeStruct((T, rd), F32),
                   jax.ShapeDtypeStruct((T, r + ROPE_PAD), BF16)],
        compiler_params=_params("parallel"),
        name="shared_kv",
    )(h, g, wdkv, gkv, wkr, wkrs, cos, sin)


def _q_kernel(h_ref, g_ref, wdq_ref, gq_ref, wn_ref, wr_ref, wrs_ref, wuk_ref, cos_ref, sin_ref,
              o_ref, *, scale):
    xn = _rms(h_ref[...], g_ref[...]).astype(BF16)
    cq = _rms(_dot(xn, wdq_ref[...]), gq_ref[...]).astype(BF16)
    qn = (_dot(cq, wn_ref[...]) * scale).astype(BF16)
    qr = ((_dot(cq, wr_ref[...]) * cos_ref[...] + _dot(cq, wrs_ref[...]) * sin_ref[...])
          * scale).astype(BF16)
    nh, nd, r = wuk_ref.shape
    rd = qr.shape[1] // nh
    for h in range(nh):
        o_ref[h, :, :r] = _dot(qn[:, h * nd:(h + 1) * nd], wuk_ref[h]).astype(BF16)
        o_ref[h, :, r:r + rd] = qr[:, h * rd:(h + 1) * rd]
        o_ref[h, :, r + rd:] = jnp.zeros((qr.shape[0], o_ref.shape[2] - r - rd), BF16)


def _mla_q(h, g, wdq, gq, wn, wr, wrs, wuk, cos, sin, tm, scale):
    T, D = h.shape
    nh, _, r = wuk.shape
    npos = cos.shape[0] // tm
    vec = lambda n: pl.BlockSpec((1, n), lambda i: (0, 0))
    mat = lambda a: pl.BlockSpec(a.shape, lambda i: (0,) * a.ndim)
    tab = pl.BlockSpec((tm, cos.shape[1]), lambda i: (i % npos, 0))
    return pl.pallas_call(
        functools.partial(_q_kernel, scale=scale),
        grid=(T // tm,),
        in_specs=[pl.BlockSpec((tm, D), lambda i: (i, 0)), vec(D), mat(wdq), vec(wdq.shape[1]),
                  mat(wn), mat(wr), mat(wrs), mat(wuk), tab, tab],
        out_specs=pl.BlockSpec((nh, tm, r + ROPE_PAD), lambda i: (0, i, 0)),
        out_shape=jax.ShapeDtypeStruct((nh, T, r + ROPE_PAD), BF16),
        compiler_params=_params("parallel"),
        name="mla_q",
    )(h, g, wdq, gq, wn, wr, wrs, wuk, cos, sin)


def _attn_kernel(q_ref, kv_ref, wuv_ref, o_ref, *, seq_len, r):
    i = pl.program_id(1)
    nh = q_ref.shape[0]
    tq = ATTN_TILE
    n_full = seq_len // tq
    tail = seq_len - n_full * tq
    vd = wuv_ref.shape[1] // nh

    def run(rows):
        m_rows = nh * rows
        q = q_ref[:, 0, :rows, :].reshape(m_rows, q_ref.shape[-1])

        def step(carry, kv, mask):
            m, l, acc = carry
            s = lax.dot_general(q, kv, NT, preferred_element_type=F32)
            if mask is not None:
                s = jnp.where(mask, s, MASK_VALUE)
            m_new = jnp.maximum(m, jnp.max(s, axis=-1, keepdims=True))
            alpha = jnp.exp(m - m_new)
            p = jnp.exp(s - m_new)
            l = alpha * l + jnp.sum(p, axis=-1, keepdims=True)
            acc = alpha * acc + _dot(p.astype(BF16), kv[:, :r])
            return m_new, l, acc

        init = (jnp.full((m_rows, 1), MASK_VALUE, F32), jnp.zeros((m_rows, 1), F32),
                jnp.zeros((m_rows, r), F32))
        carry = lax.fori_loop(
            0, i,
            lambda j, c: step(c, kv_ref[0, pl.ds(pl.multiple_of(j * tq, tq), tq), :], None),
            init)
        t_loc = lax.broadcasted_iota(jnp.int32, (m_rows, rows), 0) % rows
        s_loc = lax.broadcasted_iota(jnp.int32, (m_rows, rows), 1)
        kv_diag = kv_ref[0, pl.ds(pl.multiple_of(i * tq, tq), rows), :]
        _, l, acc = step(carry, kv_diag, s_loc <= t_loc)
        o_lat = (acc / l).astype(BF16)
        for h in range(nh):
            o_ref[0, :rows, h * vd:(h + 1) * vd] = _dot(
                o_lat[h * rows:(h + 1) * rows], wuv_ref[:, h * vd:(h + 1) * vd]).astype(o_ref.dtype)

    if n_full:
        pl.when(i < n_full)(lambda: run(tq))
    if tail:
        pl.when(i == n_full)(lambda: run(tail))


def _attn_prompt(qcat, kvb, wuv, batch, seq_len):
    nh, T, w = qcat.shape
    r = w - ROPE_PAD
    D = wuv.shape[1]
    nq = pl.cdiv(seq_len, ATTN_TILE)
    out = pl.pallas_call(
        functools.partial(_attn_kernel, seq_len=seq_len, r=r),
        grid=(batch, nq),
        in_specs=[pl.BlockSpec((nh, 1, ATTN_TILE, w), lambda b, i: (0, b, i, 0)),
                  pl.BlockSpec((1, seq_len, w), lambda b, i: (b, 0, 0)),
                  pl.BlockSpec(wuv.shape, lambda b, i: (0, 0))],
        out_specs=pl.BlockSpec((1, ATTN_TILE, D), lambda b, i: (b, i, 0)),
        out_shape=jax.ShapeDtypeStruct((batch, seq_len, D), BF16),
        compiler_params=_params("parallel", "parallel"),
        name="attn_prompt",
    )(qcat.reshape(nh, batch, seq_len, w), kvb.reshape(batch, seq_len, w), wuv)
    return out.reshape(T, D)


def _attn_dec_kernel(pt_ref, q_ref, kvn_ref, wuv_ref, *rest, n_pg, r, rd):
    ckv_refs = rest[:n_pg]
    krt_refs = rest[n_pg:2 * n_pg]
    o_ref, kv_scr, krt_scr = rest[2 * n_pg:]
    page = ckv_refs[0].shape[1]
    for p in range(n_pg):
        kv_scr[p * page:(p + 1) * page, :] = ckv_refs[p][0].astype(BF16)
        krt_scr[:, p * page:(p + 1) * page] = krt_refs[p][0].astype(BF16)
    q = q_ref[0]
    s = (lax.dot_general(q[:, :r], kv_scr[...], NT, preferred_element_type=F32)
         + _dot(q[:, r:r + rd], krt_scr[...]))
    kvn = kvn_ref[0].astype(F32)
    s_new = jnp.sum(q.astype(F32) * kvn, axis=-1, keepdims=True)
    m = jnp.maximum(jnp.max(s, axis=-1, keepdims=True), s_new)
    p_past = jnp.exp(s - m)
    p_new = jnp.exp(s_new - m)
    l = jnp.sum(p_past, axis=-1, keepdims=True) + p_new
    o_lat = (_dot(p_past.astype(BF16), kv_scr[...]) + p_new * kvn[:, :r]) / l
    full = _dot(o_lat.astype(BF16), wuv_ref[...])
    nh = full.shape[0]
    vd = full.shape[1] // nh
    head_of_lane = lax.broadcasted_iota(jnp.int32, full.shape, 1) // vd
    head_of_row = lax.broadcasted_iota(jnp.int32, full.shape, 0)
    o_ref[0] = jnp.sum(jnp.where(head_of_lane == head_of_row, full, 0.0), axis=0,
                       keepdims=True).astype(o_ref.dtype)


def _attn_decode(qcat, kvb_new, cache_ckv, cache_krt, page_table, wuv):
    nh, nseq, w = qcat.shape
    r = w - ROPE_PAD
    rd, page = cache_krt.shape[1:]
    n_pg = page_table.shape[1]
    D = wuv.shape[1]
    q = jnp.transpose(qcat, (1, 0, 2))

    def page_spec(shape, p):
        return pl.BlockSpec((1,) + shape, lambda b, pt: (pt[b, p], 0, 0))

    grid_spec = pltpu.PrefetchScalarGridSpec(
        num_scalar_prefetch=1,
        grid=(nseq,),
        in_specs=([pl.BlockSpec((1, nh, w), lambda b, pt: (b, 0, 0)),
                   pl.BlockSpec((1, 1, w), lambda b, pt: (b, 0, 0)),
                   pl.BlockSpec(wuv.shape, lambda b, pt: (0, 0))]
                  + [page_spec((page, r), p) for p in range(n_pg)]
                  + [page_spec((rd, page), p) for p in range(n_pg)]),
        out_specs=pl.BlockSpec((1, 1, D), lambda b, pt: (b, 0, 0)),
        scratch_shapes=[pltpu.VMEM((n_pg * page, r), BF16), pltpu.VMEM((rd, n_pg * page), BF16)],
    )
    out = pl.pallas_call(
        functools.partial(_attn_dec_kernel, n_pg=n_pg, r=r, rd=rd),
        grid_spec=grid_spec,
        out_shape=jax.ShapeDtypeStruct((nseq, 1, D), BF16),
        compiler_params=_params("parallel"),
        name="attn_decode",
    )(page_table, q, kvb_new.reshape(nseq, 1, w), wuv,
      *([cache_ckv] * n_pg), *([cache_krt] * n_pg))
    return out.reshape(nseq, D)


def _rope_tables(pos, rope_dim):
    half = rope_dim // 2
    inv_freq = ROPE_THETA ** (-jnp.arange(half, dtype=F32) / half)
    ang = pos.astype(F32)[:, None] * inv_freq[None, :]
    cos, sin = jnp.cos(ang), jnp.sin(ang)
    return jnp.concatenate([cos, cos], axis=1), jnp.concatenate([-sin, sin], axis=1)


def _swap_halves(w, rope_dim):
    lead = w.shape[0]
    w3 = w.reshape(lead, -1, rope_dim)
    half = rope_dim // 2
    return jnp.concatenate([w3[..., half:], w3[..., :half]], axis=-1).reshape(w.shape)


def _prep_weights(p):
    bf = lambda a: a.astype(BF16)
    depth = p["norm_mix"].shape[0]
    n_a = p["hg_wq"].shape[0]
    r, nh, nd = p["w_uk"].shape
    rope_dim = p["w_kr"].shape[1]
    lbs = jax.nn.softmax(p["hg_lower_bounds"].astype(F32), axis=0)
    lbs = jnp.cumsum(lbs, axis=0) - lbs[0]
    w_uq = p["w_uq"]
    nb = w_uq.shape[0]
    wn = w_uq[..., :nd].reshape(nb, w_uq.shape[1], nh * nd)
    wr = w_uq[..., nd:].reshape(nb, w_uq.shape[1], nh * rope_dim)
    return dict(
        depth=depth, n_a=n_a, rope_dim=rope_dim, nope_dim=nd,
        norm_mix=p["norm_mix"][:, None, :], norm_mlp=p["norm_mlp"][:, None, :],
        norm_final=p["norm_final"][None, :],
        hg_wq=bf(p["hg_wq"]), hg_wf=bf(p["hg_wf"]), hg_wi=bf(p["hg_wi"]), hg_wg=bf(p["hg_wg"]),
        hg_wo=bf(p["hg_wo"]), hg_gnorm=p["hg_gnorm"][:, None, :], lbs=lbs[:, None, :],
        norm_kv=p["norm_kv"][None, :], w_dkv=bf(p["w_dkv"]), kv_norm=p["kv_norm"][None, :],
        w_kr=bf(p["w_kr"]), w_kr_sw=bf(_swap_halves(p["w_kr"], rope_dim)),
        w_ukt=bf(jnp.transpose(p["w_uk"], (1, 2, 0))),
        w_uv=bf(p["w_uv"].reshape(r, -1)),
        w_dq=bf(p["w_dq"]), q_norm=p["q_norm"][:, None, :],
        w_uq_n=bf(wn), w_uq_r=bf(wr),
        w_uq_rs=bf(jnp.stack([_swap_halves(wr[j], rope_dim) for j in range(nb)])),
        w_o=bf(p["w_o"]), w_up=bf(p["w_up"]), w_down=bf(p["w_down"]),
    )


def _trunk(h, pos_tile, tm, w, hg_rec_fn, attend_fn, act_dtype):
    rope_dim = w["rope_dim"]
    cos, sin = _rope_tables(pos_tile, rope_dim)
    cos_q, sin_q = jnp.tile(cos, (1, MLA_HEADS)), jnp.tile(sin, (1, MLA_HEADS))
    scale = float((w["nope_dim"] + rope_dim) ** -0.5)
    depth, n_a = w["depth"], w["n_a"]
    states = []
    ckv = krope = kvb = None
    for layer in range(depth):
        last = layer == depth - 1
        if layer < n_a:
            q, lf, k, v, gate = _hg_proj(h, w["norm_mix"][layer], w["hg_wq"][layer], w["hg_wf"][layer],
                                         w["hg_wi"][layer], w["hg_wg"][layer], w["lbs"][layer],
                                         tm, act_dtype)
            mix, s_new = hg_rec_fn(layer, q, lf, k, v, gate, w["hg_gnorm"][layer])
            states.append(s_new)
            w_mix = w["hg_wo"][layer]
        else:
            j = layer - n_a
            qcat = _mla_q(h, w["norm_mix"][layer], w["w_dq"][j], w["q_norm"][j], w["w_uq_n"][j],
                          w["w_uq_r"][j], w["w_uq_rs"][j], w["w_ukt"], cos_q, sin_q, tm, scale)
            mix = attend_fn(qcat, kvb)
            w_mix = w["w_o"][j]
        h = _mlp(h, mix, w_mix, w["norm_mlp"][layer], w["w_up"][layer], w["w_down"][layer],
                 w["norm_final"], tm, final_norm=last)
        if layer == n_a - 1:
            ckv, krope, kvb = _kv(h, w["norm_kv"], w["w_dkv"], w["kv_norm"], w["w_kr"], w["w_kr_sw"],
                                  cos, sin, tm)
    return h, jnp.stack(states), ckv, krope


def kernel(x_prompt, x_sample, state_hgrn, cache_ckv, cache_krope, page_table, meta_tokens, norm_mix, norm_mlp, norm_final, hg_wq, hg_wf, hg_wi, hg_wg, hg_gnorm, hg_wo, hg_lower_bounds, norm_kv, w_dkv, kv_norm, w_kr, w_uk, w_uv, w_dq, q_norm, w_uq, w_o, w_up, w_down):
    w = _prep_weights(dict(
        norm_mix=norm_mix, norm_mlp=norm_mlp, norm_final=norm_final, hg_wq=hg_wq, hg_wf=hg_wf,
        hg_wi=hg_wi, hg_wg=hg_wg, hg_gnorm=hg_gnorm, hg_wo=hg_wo, hg_lower_bounds=hg_lower_bounds,
        norm_kv=norm_kv, w_dkv=w_dkv, kv_norm=kv_norm, w_kr=w_kr, w_uk=w_uk, w_uv=w_uv, w_dq=w_dq,
        q_norm=q_norm, w_uq=w_uq, w_o=w_o, w_up=w_up, w_down=w_down))

    b_p, seq, d = x_prompt.shape
    n_meta = meta_tokens.shape[0]
    seq_len = n_meta + seq
    meta = jnp.broadcast_to(meta_tokens.astype(x_prompt.dtype)[None], (b_p, n_meta, d))
    h_p = jnp.concatenate([meta, x_prompt], axis=1).reshape(b_p * seq_len, d)
    tm_p = _row_tile(seq_len)
    out_p, st_p, ckv_p, kr_p = _trunk(
        h_p, jnp.arange(seq_len, dtype=jnp.int32), tm_p, w,
        lambda layer, q, lf, k, v, gate, gn: _hg_rec(q, lf, k, v, gate, gn, b_p, seq_len),
        lambda qcat, kvb: _attn_prompt(qcat, kvb, w["w_uv"], b_p, seq_len),
        BF16)
    y_prompt = out_p.reshape(b_p, seq_len, d)[:, n_meta:]
    ckv_prompt = ckv_p.reshape(b_p, seq_len, -1)
    krope_prompt = kr_p.reshape(b_p, seq_len, -1)

    b_s, dec, _ = x_sample.shape
    assert dec == 1
    past_len = page_table.shape[1] * cache_ckv.shape[1]
    pos_s = jnp.full((b_s,), past_len, dtype=jnp.int32)
    cache_krt = jnp.transpose(cache_krope, (0, 2, 1))
    out_s, st_s, ckv_s, kr_s = _trunk(
        x_sample.reshape(b_s, d), pos_s, b_s, w,
        lambda layer, q, lf, k, v, gate, gn: _hg_dec(q, lf, k, v, gate, gn, state_hgrn[layer]),
        lambda qcat, kvb: _attn_decode(qcat, kvb, cache_ckv, cache_krt, page_table, w["w_uv"]),
        F32)
    return (y_prompt, out_s.reshape(b_s, dec, d), st_p, ckv_prompt, krope_prompt,
            st_s.astype(state_hgrn.dtype), ckv_s.reshape(b_s, dec, -1), kr_s.reshape(b_s, dec, -1))
```

```python
import functools

import jax
import jax.numpy as jnp
from jax import lax
from jax.experimental import pallas as pl
from jax.experimental.pallas import tpu as pltpu

F32 = jnp.float32
BF16 = jnp.bfloat16

EPS = 1e-6
LB_FLOOR = 1e-30
ROPE_THETA = 10000.0
MASK_VALUE = -1e30

HG_HEADS = 8
MLA_HEADS = 8
HG_CHUNK = 64
HG_SUB = 16
HG_HEADS_PER_STEP = 4
SUBLANES = 8
LOG2E = 1.4426950408889634
ATTN_TILE = 256
ROPE_PAD = 128
DEC_STATE_BLOCK = 4
VMEM_LIMIT_BYTES = 56 * 1024 * 1024

NT = (((1,), (1,)), ((), ()))
TN = (((0,), (0,)), ((), ()))


def _params(*sem):
    return pltpu.CompilerParams(dimension_semantics=sem, vmem_limit_bytes=VMEM_LIMIT_BYTES)


def _rms(x, g):
    return x * lax.rsqrt(jnp.mean(x * x, axis=-1, keepdims=True) + EPS) * g


def _sigmoid(x):
    return 1.0 / (1.0 + jnp.exp(-x))


def _dot(a, b):
    return jnp.dot(a, b, preferred_element_type=F32)


def _row_tile(seq_len, cap=1024):
    best = None
    for t in range(16, min(seq_len, cap) + 1, 16):
        if seq_len % t == 0:
            best = t
    assert best is not None, seq_len
    return best


def _hg_proj_kernel(h_ref, g_ref, wq_ref, wf_ref, wi_ref, wg_ref, lb_ref,
                    q_ref, lf_ref, k_ref, v_ref, gate_ref):
    xn = _rms(h_ref[...], g_ref[...]).astype(BF16)
    q = _dot(xn, wq_ref[...])
    q_ref[...] = (q * _sigmoid(q)).astype(q_ref.dtype)
    z = _dot(xn, wf_ref[...])
    lb = lb_ref[...]
    log_1m_lb = jnp.log1p(-lb)
    log_lb = jnp.log(jnp.maximum(lb, LB_FLOOR))
    t = jnp.exp(-jnp.abs(z))
    a = log_1m_lb + (jnp.minimum(z, 0.0) - jnp.log1p(t))
    lf_ref[...] = jnp.maximum(a, log_lb) + jnp.log1p(jnp.exp(-jnp.abs(a - log_lb)))
    k_ref[...] = ((1.0 - lb) * (jnp.where(z >= 0.0, t, 1.0) / (1.0 + t))).astype(k_ref.dtype)
    v_ref[...] = _dot(xn, wi_ref[...]).astype(v_ref.dtype)
    gt = _dot(xn, wg_ref[...])
    gate_ref[...] = (gt * _sigmoid(gt)).astype(gate_ref.dtype)


def _hg_proj(h, g, wq, wf, wi, wg, lb, tm, act_dtype):
    T, D = h.shape
    N = wq.shape[1]
    row = pl.BlockSpec((tm, D), lambda i: (i, 0))
    vec = lambda n: pl.BlockSpec((1, n), lambda i: (0, 0))
    mat = lambda a: pl.BlockSpec(a.shape, lambda i: (0, 0))
    out = lambda n: pl.BlockSpec((tm, n), lambda i: (i, 0))
    return pl.pallas_call(
        _hg_proj_kernel,
        grid=(T // tm,),
        in_specs=[row, vec(D), mat(wq), mat(wf), mat(wi), mat(wg), vec(N)],
        out_specs=[out(N), out(N), out(N), out(D), out(D)],
        out_shape=[jax.ShapeDtypeStruct((T, N), act_dtype), jax.ShapeDtypeStruct((T, N), F32),
                   jax.ShapeDtypeStruct((T, N), act_dtype), jax.ShapeDtypeStruct((T, D), act_dtype),
                   jax.ShapeDtypeStruct((T, D), act_dtype)],
        compiler_params=_params("parallel"),
        name="hg_proj",
    )(h, g, wq, wf, wi, wg, lb)


def _hg_rec_kernel(q_ref, lf_ref, k_ref, v_ref, gate_ref, gn_ref, o_ref, s_ref, st_scr,
                   *, seq_len, heads):
    st_scr[...] = jnp.zeros_like(st_scr)
    kd = q_ref.shape[-1] // heads
    vd = v_ref.shape[-1] // heads
    ones_kk = jnp.ones((kd, kd), BF16)
    t_idx = lax.broadcasted_iota(jnp.int32, (SUBLANES, 1), 0)

    def head_chunk(g, r0, C):
        rows = pl.ds(r0, C)
        kl = slice(g * kd, (g + 1) * kd)
        vl = slice(g * vd, (g + 1) * vd)
        n_sub = C // HG_SUB
        lf = lf_ref[0, rows, kl] * LOG2E
        tri = (lax.broadcasted_iota(jnp.int32, (C, C), 0)
               >= lax.broadcasted_iota(jnp.int32, (C, C), 1)).astype(BF16)
        hi = lf.astype(BF16)
        r1 = lf - hi.astype(F32)
        mid = r1.astype(BF16)
        lo = (r1 - mid.astype(F32)).astype(BF16)
        b = _dot(tri, hi) + _dot(tri, mid) + _dot(tri, lo)
        q = q_ref[0, rows, kl].astype(F32)
        k = k_ref[0, rows, kl].astype(F32)
        vb = v_ref[0, rows, vl].astype(BF16)
        vf = vb.astype(F32)
        yield
        st = st_scr[g]
        o = lax.dot_general((q * jnp.exp2(b)).astype(BF16), st.astype(BF16), NT,
                            preferred_element_type=F32)
        scores = []
        for i in range(1, n_sub):
            lo_r = i * HG_SUB
            anchor = b[lo_r - 1:lo_r]
            qd = (q[lo_r:lo_r + HG_SUB] * jnp.exp2(b[lo_r:lo_r + HG_SUB] - anchor)).astype(BF16)
            kdec = (k[:lo_r] * jnp.exp2(anchor - b[:lo_r])).astype(BF16)
            scores.append(lax.dot_general(qd, kdec, NT, preferred_element_type=F32))
        b_last = b[C - 1:C]
        kdec = (k * jnp.exp2(b_last - b)).astype(BF16)
        st_scr[g] = st * jnp.exp2(b_last) + lax.dot_general(vb, kdec, TN,
                                                            preferred_element_type=F32)
        yield
        off = [None] + [_dot(sc.astype(BF16), vb[:(i + 1) * HG_SUB])
                        for i, sc in enumerate(scores)]
        sums = []
        for i in range(n_sub):
            lo_r, hi_r = i * HG_SUB, (i + 1) * HG_SUB
            b_i, q_i, k_i = b[lo_r:hi_r], q[lo_r:hi_r], k[lo_r:hi_r]
            prods = []
            for s in range(HG_SUB):
                for p0 in range((s // SUBLANES) * SUBLANES, HG_SUB, SUBLANES):
                    e = jnp.exp2(b_i[p0:p0 + SUBLANES] - b_i[s:s + 1])
                    if p0 <= s:
                        e = jnp.where(t_idx >= s - p0, e, 0.0)
                    prods.append(q_i[p0:p0 + SUBLANES] * e * k_i[s:s + 1])
            sums.append(_dot(jnp.concatenate(prods, axis=0).astype(BF16), ones_kk))
            yield
        parts = []
        for i in range(n_sub):
            lo_r = i * HG_SUB
            v_i = vf[lo_r:lo_r + HG_SUB]
            acc = {p0: (jnp.zeros((SUBLANES, vd), F32) if off[i] is None
                        else off[i][p0:p0 + SUBLANES]) for p0 in range(0, HG_SUB, SUBLANES)}
            n = 0
            for s in range(HG_SUB):
                for p0 in range((s // SUBLANES) * SUBLANES, HG_SUB, SUBLANES):
                    acc[p0] = acc[p0] + sums[i][n * SUBLANES:(n + 1) * SUBLANES] * v_i[s:s + 1]
                    n += 1
            parts.extend(acc[p0] for p0 in sorted(acc))
        o = o + jnp.concatenate(parts, axis=0)
        gate = gate_ref[0, rows, vl].astype(F32)
        o_ref[0, rows, vl] = (_rms(o, gn_ref[:, vl]) * gate).astype(o_ref.dtype)

    def chunk(r0, C):
        live = [head_chunk(g, r0, C) for g in range(heads)]
        while live:
            live = [gen for gen in live if next(gen, True) is None]

    n_full = seq_len // HG_CHUNK
    tail = seq_len - n_full * HG_CHUNK

    def body(c, carry):
        chunk(pl.multiple_of(c * HG_CHUNK, HG_CHUNK), HG_CHUNK)
        return carry

    lax.fori_loop(0, n_full, body, 0)
    if tail:
        chunk(n_full * HG_CHUNK, tail)
    for g in range(heads):
        s_ref[0, g] = st_scr[g].T


def _hg_rec(q, lf, k, v, gate, gn, batch, seq_len):
    T, N = q.shape
    D = v.shape[1]
    kd, vd = N // HG_HEADS, D // HG_HEADS
    G = HG_HEADS_PER_STEP
    assert seq_len % HG_SUB == 0 and HG_HEADS % G == 0
    r3 = lambda a: a.reshape(batch, seq_len, a.shape[1])
    seq_k = pl.BlockSpec((1, seq_len, G * kd), lambda b, h: (b, 0, h))
    seq_v = pl.BlockSpec((1, seq_len, G * vd), lambda b, h: (b, 0, h))
    o, s = pl.pallas_call(
        functools.partial(_hg_rec_kernel, seq_len=seq_len, heads=G),
        grid=(batch, HG_HEADS // G),
        in_specs=[seq_k, seq_k, seq_k, seq_v, seq_v,
                  pl.BlockSpec((1, G * vd), lambda b, h: (0, h))],
        out_specs=[seq_v, pl.BlockSpec((1, G, kd, vd), lambda b, h: (b, h, 0, 0))],
        out_shape=[jax.ShapeDtypeStruct((batch, seq_len, D), BF16),
                   jax.ShapeDtypeStruct((batch, HG_HEADS, kd, vd), F32)],
        scratch_shapes=[pltpu.VMEM((G, vd, kd), F32)],
        compiler_params=_params("parallel", "parallel"),
        name="hg_rec",
    )(r3(q), r3(lf), r3(k), r3(v), r3(gate), gn)
    return o.reshape(T, D), s


def _hg_dec_kernel(q_ref, lf_ref, k_ref, v_ref, gate_ref, gn_ref, s0_ref, *rest, out_layer):
    o_ref, s_ref = rest[-2:]
    _, nb, nh, _, vd = s0_ref.shape
    for other in range(s_ref.shape[0]):
        if other != out_layer:
            s_ref[other] = jnp.zeros(s_ref.shape[1:], s_ref.dtype)
    for b in range(nb):
        outs = []
        for h in range(nh):
            f_col = jnp.exp(lf_ref[b, :, h:h + 1])
            k_col = k_ref[b, :, h:h + 1]
            q_col = q_ref[b, :, h:h + 1]
            v_row = v_ref[b, :, h * vd:(h + 1) * vd]
            s_new = f_col * s0_ref[0, b, h] + k_col * v_row
            s_ref[out_layer, b, h] = s_new
            o = jnp.sum(s_new * q_col, axis=0, keepdims=True)
            outs.append(_rms(o, gn_ref[:, h * vd:(h + 1) * vd]))
        o_all = jnp.concatenate(outs, axis=1) * gate_ref[b]
        o_ref[b] = o_all.astype(o_ref.dtype)


def _hg_dec(q, lf, k, v, gate, gn, state_all, layer, new_state_all):
    n_layers, nseq, nh, kd, vd = state_all.shape
    D = nh * vd
    nb = DEC_STATE_BLOCK if nseq % DEC_STATE_BLOCK == 0 else 1
    km = lambda a: jnp.transpose(a.reshape(nseq, nh, kd), (0, 2, 1))
    col = pl.BlockSpec((nb, kd, nh), lambda i: (i, 0, 0))
    rowv = pl.BlockSpec((nb, 1, D), lambda i: (i, 0, 0))
    st = pl.BlockSpec((1, nb, nh, kd, vd), lambda i: (layer, i, 0, 0, 0))
    in_specs = [col, col, col, rowv, rowv, pl.BlockSpec((1, D), lambda i: (0, 0)), st]
    args = [km(q), km(lf), km(k), v.reshape(nseq, 1, D), gate.reshape(nseq, 1, D), gn, state_all]
    if new_state_all is None:
        st_out, out_layer, aliases = pl.BlockSpec((n_layers, nb, nh, kd, vd),
                                                  lambda i: (0, i, 0, 0, 0)), layer, {}
    else:
        in_specs.append(pl.BlockSpec(memory_space=pl.ANY))
        args.append(new_state_all)
        st_out, out_layer, aliases = st, 0, {len(args) - 1: 1}
    o, s = pl.pallas_call(
        functools.partial(_hg_dec_kernel, out_layer=out_layer),
        grid=(nseq // nb,),
        in_specs=in_specs,
        out_specs=[rowv, st_out],
        out_shape=[jax.ShapeDtypeStruct((nseq, 1, D), BF16),
                   jax.ShapeDtypeStruct(state_all.shape, F32)],
        input_output_aliases=aliases,
        compiler_params=_params("parallel"),
        name="hg_dec",
    )(*args)
    return o.reshape(nseq, D), s


def _mlp_kernel(h_ref, mix_ref, wmix_ref, g_ref, wup_ref, wdn_ref, gf_ref, o_ref, xn_scr,
                *, final_norm):
    j = pl.program_id(1)

    @pl.when(j == 0)
    def _():
        h1 = h_ref[...] + _dot(mix_ref[...], wmix_ref[...])
        o_ref[...] = h1
        xn_scr[...] = _rms(h1, g_ref[...]).astype(BF16)

    u = jnp.maximum(_dot(xn_scr[...], wup_ref[...]), 0.0)
    o_ref[...] += _dot((u * u).astype(BF16), wdn_ref[...])

    if final_norm:
        @pl.when(j == pl.num_programs(1) - 1)
        def _():
            o_ref[...] = _rms(o_ref[...], gf_ref[...])


def _mlp(h, mix, wmix, g, wup, wdn, gf, tm, final_norm):
    T, D = h.shape
    dff = wup.shape[1]
    tf = min(dff, 1024)
    row = lambda: pl.BlockSpec((tm, D), lambda i, j: (i, 0))
    vec = lambda: pl.BlockSpec((1, D), lambda i, j: (0, 0))
    return pl.pallas_call(
        functools.partial(_mlp_kernel, final_norm=final_norm),
        grid=(T // tm, dff // tf),
        in_specs=[row(), row(), pl.BlockSpec((D, D), lambda i, j: (0, 0)), vec(),
                  pl.BlockSpec((D, tf), lambda i, j: (0, j)),
                  pl.BlockSpec((tf, D), lambda i, j: (j, 0)), vec()],
        out_specs=row(),
        out_shape=jax.ShapeDtypeStruct((T, D), F32),
        scratch_shapes=[pltpu.VMEM((tm, D), BF16)],
        compiler_params=_params("parallel", "arbitrary"),
        name="mlp",
    )(h, mix, wmix, g, wup, wdn, gf)


def _kv_kernel(h_ref, g_ref, wdkv_ref, gkv_ref, wkr_ref, wkrs_ref, cos_ref, sin_ref,
               ckv_ref, kr_ref, kvb_ref):
    hn = _rms(h_ref[...], g_ref[...]).astype(BF16)
    ckv = _rms(_dot(hn, wdkv_ref[...]), gkv_ref[...])
    kr = _dot(hn, wkr_ref[...]) * cos_ref[...] + _dot(hn, wkrs_ref[...]) * sin_ref[...]
    ckv_ref[...] = ckv
    kr_ref[...] = kr
    r, rd = ckv.shape[1], kr.shape[1]
    kvb_ref[:, :r] = ckv.astype(BF16)
    kvb_ref[:, r:r + rd] = kr.astype(BF16)
    kvb_ref[:, r + rd:] = jnp.zeros((kr.shape[0], kvb_ref.shape[1] - r - rd), BF16)


def _kv(h, g, wdkv, gkv, wkr, wkrs, cos, sin, tm):
    T, D = h.shape
    r, rd = wdkv.shape[1], wkr.shape[1]
    npos = cos.shape[0] // tm
    row = lambda n: pl.BlockSpec((tm, n), lambda i: (i, 0))
    vec = lambda n: pl.BlockSpec((1, n), lambda i: (0, 0))
    mat = lambda a: pl.BlockSpec(a.shape, lambda i: (0, 0))
    tab = pl.BlockSpec((tm, rd), lambda i: (i % npos, 0))
    return pl.pallas_call(
        _kv_kernel,
        grid=(T // tm,),
        in_specs=[row(D), vec(D), mat(wdkv), vec(r), mat(wkr), mat(wkrs), tab, tab],
        out_specs=[row(r), row(rd), row(r + ROPE_PAD)],
        out_shape=[jax.ShapeDtypeStruct((T, r), F32), jax.ShapeDtypeStruct((T, rd), F32),
                   jax.ShapeDtypeStruct((T, r + ROPE_PAD), BF16)],
        compiler_params=_params("parallel"),
        name="shared_kv",
    )(h, g, wdkv, gkv, wkr, wkrs, cos, sin)


def _q_kernel(h_ref, g_ref, wdq_ref, gq_ref, wn_ref, wr_ref, wrs_ref, wuk_ref, cos_ref, sin_ref,
              o_ref, *, scale):
    xn = _rms(h_ref[...], g_ref[...]).astype(BF16)
    cq = _rms(_dot(xn, wdq_ref[...]), gq_ref[...]).astype(BF16)
    qn = (_dot(cq, wn_ref[...]) * scale).astype(BF16)
    qr = ((_dot(cq, wr_ref[...]) * cos_ref[...] + _dot(cq, wrs_ref[...]) * sin_ref[...])
          * scale).astype(BF16)
    nh, nd, r = wuk_ref.shape
    rd = qr.shape[1] // nh
    for h in range(nh):
        o_ref[h, :, :r] = _dot(qn[:, h * nd:(h + 1) * nd], wuk_ref[h]).astype(BF16)
        o_ref[h, :, r:r + rd] = qr[:, h * rd:(h + 1) * rd]
        o_ref[h, :, r + rd:] = jnp.zeros((qr.shape[0], o_ref.shape[2] - r - rd), BF16)


def _mla_q(h, g, wdq, gq, wn, wr, wrs, wuk, cos, sin, tm, scale):
    T, D = h.shape
    nh, _, r = wuk.shape
    npos = cos.shape[0] // tm
    vec = lambda n: pl.BlockSpec((1, n), lambda i: (0, 0))
    mat = lambda a: pl.BlockSpec(a.shape, lambda i: (0,) * a.ndim)
    tab = pl.BlockSpec((tm, cos.shape[1]), lambda i: (i % npos, 0))
    return pl.pallas_call(
        functools.partial(_q_kernel, scale=scale),
        grid=(T // tm,),
        in_specs=[pl.BlockSpec((tm, D), lambda i: (i, 0)), vec(D), mat(wdq), vec(wdq.shape[1]),
                  mat(wn), mat(wr), mat(wrs), mat(wuk), tab, tab],
        out_specs=pl.BlockSpec((nh, tm, r + ROPE_PAD), lambda i: (0, i, 0)),
        out_shape=jax.ShapeDtypeStruct((nh, T, r + ROPE_PAD), BF16),
        compiler_params=_params("parallel"),
        name="mla_q",
    )(h, g, wdq, gq, wn, wr, wrs, wuk, cos, sin)


def _run_staggered(gens):
    started, nxt = [], 0
    while nxt < len(gens) or started:
        if nxt < len(gens):
            started.insert(0, gens[nxt])
            nxt += 1
        started = [g for g in started if next(g, True) is None]


def _attn_kernel(q_ref, kv_ref, kvt_ref, wuv_ref, o_ref, m_scr, l_scr, acc_scr, *, seq_len, r):
    i = pl.program_id(1)
    nh = q_ref.shape[0]
    tq = ATTN_TILE
    n_full = seq_len // tq
    tail = seq_len - n_full * tq
    vd = wuv_ref.shape[1] // nh

    def run_full():
        m_scr[...] = jnp.full_like(m_scr, MASK_VALUE)
        l_scr[...] = jnp.zeros_like(l_scr)
        acc_scr[...] = jnp.zeros_like(acc_scr)

        def scores(h, j):
            kv = kv_ref[0, pl.ds(pl.multiple_of(j * tq, tq), tq), :]
            return lax.dot_general(kv, q_ref[h, 0], NT, preferred_element_type=F32)

        def head_block(h, j, s, s_next, mask):
            if s_next is not None:
                s_next[h] = scores(h, j + 1)
            yield
            if mask is not None:
                s = jnp.where(mask, s, MASK_VALUE)
            m = m_scr[h]
            m_new = jnp.maximum(m, jnp.max(s, axis=0, keepdims=True))
            alpha = jnp.exp2(m - m_new)
            p = jnp.exp2(s - m_new)
            l_scr[h] = alpha * l_scr[h] + jnp.sum(p, axis=0, keepdims=True)
            m_scr[h] = m_new
            kvt = kvt_ref[0, :, pl.ds(pl.multiple_of(j * tq, tq), tq)]
            pv = _dot(kvt, p.astype(BF16))
            yield
            acc_scr[h] = alpha * acc_scr[h] + pv

        def body(j, s_cur):
            s_next = [None] * nh
            _run_staggered([head_block(h, j, s_cur[h], s_next, None) for h in range(nh)])
            return tuple(s_next)

        s_diag = lax.fori_loop(0, i, body, tuple(scores(h, 0) for h in range(nh)))
        key_loc = lax.broadcasted_iota(jnp.int32, (tq, tq), 0)
        qry_loc = lax.broadcasted_iota(jnp.int32, (tq, tq), 1)
        _run_staggered([head_block(h, i, s_diag[h], None, key_loc <= qry_loc) for h in range(nh)])
        for h in range(nh):
            o_t = (acc_scr[h] * (1.0 / l_scr[h])).astype(BF16)
            o_ref[0, :, h * vd:(h + 1) * vd] = lax.dot_general(
                o_t, wuv_ref[:, h * vd:(h + 1) * vd], TN,
                preferred_element_type=F32).astype(o_ref.dtype)

    def run_tail(rows):
        m_rows = nh * rows
        q = q_ref[:, 0, :rows, :].reshape(m_rows, q_ref.shape[-1])

        def step(carry, kv, mask):
            m, l, acc = carry
            s = lax.dot_general(q, kv, NT, preferred_element_type=F32)
            if mask is not None:
                s = jnp.where(mask, s, MASK_VALUE)
            m_new = jnp.maximum(m, jnp.max(s, axis=-1, keepdims=True))
            alpha = jnp.exp2(m - m_new)
            p = jnp.exp2(s - m_new)
            l = alpha * l + jnp.sum(p, axis=-1, keepdims=True)
            acc = alpha * acc + _dot(p.astype(BF16), kv[:, :r])
            return m_new, l, acc

        init = (jnp.full((m_rows, 1), MASK_VALUE, F32), jnp.zeros((m_rows, 1), F32),
                jnp.zeros((m_rows, r), F32))
        carry = lax.fori_loop(
            0, n_full,
            lambda j, c: step(c, kv_ref[0, pl.ds(pl.multiple_of(j * tq, tq), tq), :], None),
            init)
        t_loc = lax.broadcasted_iota(jnp.int32, (m_rows, rows), 0) % rows
        s_loc = lax.broadcasted_iota(jnp.int32, (m_rows, rows), 1)
        _, l, acc = step(carry, kv_ref[0, n_full * tq:n_full * tq + rows, :], s_loc <= t_loc)
        o_lat = (acc / l).astype(BF16)
        for h in range(nh):
            o_ref[0, :rows, h * vd:(h + 1) * vd] = _dot(
                o_lat[h * rows:(h + 1) * rows], wuv_ref[:, h * vd:(h + 1) * vd]).astype(o_ref.dtype)

    if n_full:
        pl.when(i < n_full)(run_full)
    if tail:
        pl.when(i == n_full)(lambda: run_tail(tail))


def _attn_prompt(qcat, kvb, wuv, batch, seq_len):
    nh, T, w = qcat.shape
    r = w - ROPE_PAD
    D = wuv.shape[1]
    tq = ATTN_TILE
    nq = pl.cdiv(seq_len, tq)
    kv3 = kvb.reshape(batch, seq_len, w)
    kvt = jnp.swapaxes(kv3[:, :, :r], 1, 2)
    out = pl.pallas_call(
        functools.partial(_attn_kernel, seq_len=seq_len, r=r),
        grid=(batch, nq),
        in_specs=[pl.BlockSpec((nh, 1, tq, w), lambda b, i: (0, b, i, 0)),
                  pl.BlockSpec((1, seq_len, w), lambda b, i: (b, 0, 0)),
                  pl.BlockSpec((1, r, seq_len), lambda b, i: (b, 0, 0)),
                  pl.BlockSpec(wuv.shape, lambda b, i: (0, 0))],
        out_specs=pl.BlockSpec((1, tq, D), lambda b, i: (b, i, 0)),
        out_shape=jax.ShapeDtypeStruct((batch, seq_len, D), BF16),
        scratch_shapes=[pltpu.VMEM((nh, 1, tq), F32), pltpu.VMEM((nh, 1, tq), F32),
                        pltpu.VMEM((nh, r, tq), F32)],
        compiler_params=_params("parallel", "parallel"),
        name="attn_prompt",
    )(qcat.reshape(nh, batch, seq_len, w), kv3, kvt, wuv)
    return out.reshape(T, D)


def _attn_dec_kernel(pt_ref, q_ref, kvn_ref, wuv_ref, ckv_hbm, krt_hbm, o_ref,
                     ckv_buf, krt_buf, kv_scr, krt_scr, sem, *, r, rd):
    b = pl.program_id(0)
    n_pg, _, page = krt_buf.shape[1:]

    def page_copies(seq, slot, p):
        idx = pt_ref[seq, p]
        rows = pl.ds(pl.multiple_of(p * page, page), page)
        return (pltpu.make_async_copy(ckv_hbm.at[idx], ckv_buf.at[slot, rows], sem.at[0, slot]),
                pltpu.make_async_copy(krt_hbm.at[idx], krt_buf.at[slot, p], sem.at[1, slot]))

    def for_pages(seq, slot, act):
        def body(p, carry):
            for c in page_copies(seq, slot, p):
                act(c)
            return carry
        lax.fori_loop(0, n_pg, body, 0)

    slot = b % 2

    @pl.when(b == 0)
    def _():
        for_pages(0, 0, lambda c: c.start())

    @pl.when(b + 1 < pl.num_programs(0))
    def _():
        for_pages(b + 1, 1 - slot, lambda c: c.start())

    for_pages(b, slot, lambda c: c.wait())

    q = q_ref[0]
    nh = q.shape[0]
    half = (n_pg // 2) * page
    s_parts = []
    for side in range(2):
        lanes = slice(side * r, (side + 1) * r)
        keys = slice(side * half, (side + 1) * half)
        kv_scr[:, lanes] = ckv_buf[slot, keys, :].astype(BF16)
        for p in range(side * (n_pg // 2), (side + 1) * (n_pg // 2)):
            krt_scr[:, p * page:(p + 1) * page] = krt_buf[slot, p].astype(BF16)
        s_parts.append(lax.dot_general(q[:, :r], kv_scr[:, lanes], NT, preferred_element_type=F32)
                       + _dot(q[:, r:r + rd], krt_scr[:, keys]))
    kvn = kvn_ref[0].astype(F32)
    s_new = jnp.sum(q.astype(F32) * kvn, axis=-1, keepdims=True)
    m = s_new
    for s in s_parts:
        m = jnp.maximum(m, jnp.max(s, axis=-1, keepdims=True))
    p_new = jnp.exp2(s_new - m)
    p_parts = [jnp.exp2(s - m) for s in s_parts]
    l = p_new + sum(jnp.sum(p, axis=-1, keepdims=True) for p in p_parts)
    pv = _dot(jnp.concatenate(p_parts, axis=0).astype(BF16), kv_scr[...])
    o_lat = (pv[:nh, :r] + pv[nh:, r:] + p_new * kvn[:, :r]) / l
    full = _dot(o_lat.astype(BF16), wuv_ref[...])
    vd = full.shape[1] // nh
    head_of_lane = lax.broadcasted_iota(jnp.int32, full.shape, 1) // vd
    head_of_row = lax.broadcasted_iota(jnp.int32, full.shape, 0)
    o_ref[0] = jnp.sum(jnp.where(head_of_lane == head_of_row, full, 0.0), axis=0,
                       keepdims=True).astype(o_ref.dtype)


def _attn_decode(qcat, kvb_new, cache_ckv, cache_krt, page_table, wuv):
    nh, nseq, w = qcat.shape
    r = w - ROPE_PAD
    rd, page = cache_krt.shape[1:]
    n_pg = page_table.shape[1]
    assert n_pg % 2 == 0
    D = wuv.shape[1]
    q = jnp.transpose(qcat, (1, 0, 2))
    grid_spec = pltpu.PrefetchScalarGridSpec(
        num_scalar_prefetch=1,
        grid=(nseq,),
        in_specs=[pl.BlockSpec((1, nh, w), lambda b, pt: (b, 0, 0)),
                  pl.BlockSpec((1, 1, w), lambda b, pt: (b, 0, 0)),
                  pl.BlockSpec(wuv.shape, lambda b, pt: (0, 0)),
                  pl.BlockSpec(memory_space=pl.ANY),
                  pl.BlockSpec(memory_space=pl.ANY)],
        out_specs=pl.BlockSpec((1, 1, D), lambda b, pt: (b, 0, 0)),
        scratch_shapes=[pltpu.VMEM((2, n_pg * page, r), F32), pltpu.VMEM((2, n_pg, rd, page), F32),
                        pltpu.VMEM((n_pg // 2 * page, 2 * r), BF16), pltpu.VMEM((rd, n_pg * page), BF16),
                        pltpu.SemaphoreType.DMA((2, 2))],
    )
    out = pl.pallas_call(
        functools.partial(_attn_dec_kernel, r=r, rd=rd),
        grid_spec=grid_spec,
        out_shape=jax.ShapeDtypeStruct((nseq, 1, D), BF16),
        compiler_params=_params("arbitrary"),
        name="attn_decode",
    )(page_table, q, kvb_new.reshape(nseq, 1, w), wuv, cache_ckv, cache_krt)
    return out.reshape(nseq, D)


def _rope_tables(pos, rope_dim):
    half = rope_dim // 2
    inv_freq = ROPE_THETA ** (-jnp.arange(half, dtype=F32) / half)
    ang = pos.astype(F32)[:, None] * inv_freq[None, :]
    cos, sin = jnp.cos(ang), jnp.sin(ang)
    return jnp.concatenate([cos, cos], axis=1), jnp.concatenate([-sin, sin], axis=1)


def _swap_halves(w, rope_dim):
    lead = w.shape[0]
    w3 = w.reshape(lead, -1, rope_dim)
    half = rope_dim // 2
    return jnp.concatenate([w3[..., half:], w3[..., :half]], axis=-1).reshape(w.shape)


def _prep_weights(p):
    bf = lambda a: a.astype(BF16)
    depth = p["norm_mix"].shape[0]
    n_a = p["hg_wq"].shape[0]
    r, nh, nd = p["w_uk"].shape
    rope_dim = p["w_kr"].shape[1]
    lbs = jax.nn.softmax(p["hg_lower_bounds"].astype(F32), axis=0)
    lbs = jnp.cumsum(lbs, axis=0) - lbs[0]
    w_uq = p["w_uq"]
    nb = w_uq.shape[0]
    wn = w_uq[..., :nd].reshape(nb, w_uq.shape[1], nh * nd)
    wr = w_uq[..., nd:].reshape(nb, w_uq.shape[1], nh * rope_dim)
    return dict(
        depth=depth, n_a=n_a, rope_dim=rope_dim, nope_dim=nd,
        norm_mix=p["norm_mix"][:, None, :], norm_mlp=p["norm_mlp"][:, None, :],
        norm_final=p["norm_final"][None, :],
        hg_wq=bf(p["hg_wq"]), hg_wf=bf(p["hg_wf"]), hg_wi=bf(p["hg_wi"]), hg_wg=bf(p["hg_wg"]),
        hg_wo=bf(p["hg_wo"]), hg_gnorm=p["hg_gnorm"][:, None, :], lbs=lbs[:, None, :],
        norm_kv=p["norm_kv"][None, :], w_dkv=bf(p["w_dkv"]), kv_norm=p["kv_norm"][None, :],
        w_kr=bf(p["w_kr"]), w_kr_sw=bf(_swap_halves(p["w_kr"], rope_dim)),
        w_ukt=bf(jnp.transpose(p["w_uk"], (1, 2, 0))),
        w_uv=bf(p["w_uv"].reshape(r, -1)),
        w_dq=bf(p["w_dq"]), q_norm=p["q_norm"][:, None, :],
        w_uq_n=bf(wn), w_uq_r=bf(wr),
        w_uq_rs=bf(jnp.stack([_swap_halves(wr[j], rope_dim) for j in range(nb)])),
        w_o=bf(p["w_o"]), w_up=bf(p["w_up"]), w_down=bf(p["w_down"]),
    )


def _trunk(h, pos_tile, tm, w, hg_rec_fn, attend_fn, act_dtype):
    rope_dim = w["rope_dim"]
    cos, sin = _rope_tables(pos_tile, rope_dim)
    cos_q, sin_q = jnp.tile(cos, (1, MLA_HEADS)), jnp.tile(sin, (1, MLA_HEADS))
    scale = float((w["nope_dim"] + rope_dim) ** -0.5) * LOG2E
    depth, n_a = w["depth"], w["n_a"]
    states = []
    ckv = krope = kvb = None
    for layer in range(depth):
        last = layer == depth - 1
        if layer < n_a:
            q, lf, k, v, gate = _hg_proj(h, w["norm_mix"][layer], w["hg_wq"][layer], w["hg_wf"][layer],
                                         w["hg_wi"][layer], w["hg_wg"][layer], w["lbs"][layer],
                                         tm, act_dtype)
            mix, s_new = hg_rec_fn(layer, q, lf, k, v, gate, w["hg_gnorm"][layer])
            states.append(s_new)
            w_mix = w["hg_wo"][layer]
        else:
            j = layer - n_a
            qcat = _mla_q(h, w["norm_mix"][layer], w["w_dq"][j], w["q_norm"][j], w["w_uq_n"][j],
                          w["w_uq_r"][j], w["w_uq_rs"][j], w["w_ukt"], cos_q, sin_q, tm, scale)
            mix = attend_fn(qcat, kvb)
            w_mix = w["w_o"][j]
        h = _mlp(h, mix, w_mix, w["norm_mlp"][layer], w["w_up"][layer], w["w_down"][layer],
                 w["norm_final"], tm, final_norm=last)
        if layer == n_a - 1:
            ckv, krope, kvb = _kv(h, w["norm_kv"], w["w_dkv"], w["kv_norm"], w["w_kr"], w["w_kr_sw"],
                                  cos, sin, tm)
    return h, states, ckv, krope


def kernel(x_prompt, x_sample, state_hgrn, cache_ckv, cache_krope, page_table, meta_tokens, norm_mix, norm_mlp, norm_final, hg_wq, hg_wf, hg_wi, hg_wg, hg_gnorm, hg_wo, hg_lower_bounds, norm_kv, w_dkv, kv_norm, w_kr, w_uk, w_uv, w_dq, q_norm, w_uq, w_o, w_up, w_down):
    w = _prep_weights(dict(
        norm_mix=norm_mix, norm_mlp=norm_mlp, norm_final=norm_final, hg_wq=hg_wq, hg_wf=hg_wf,
        hg_wi=hg_wi, hg_wg=hg_wg, hg_gnorm=hg_gnorm, hg_wo=hg_wo, hg_lower_bounds=hg_lower_bounds,
        norm_kv=norm_kv, w_dkv=w_dkv, kv_norm=kv_norm, w_kr=w_kr, w_uk=w_uk, w_uv=w_uv, w_dq=w_dq,
        q_norm=q_norm, w_uq=w_uq, w_o=w_o, w_up=w_up, w_down=w_down))

    b_s, dec, d = x_sample.shape
    assert dec == 1
    past_len = page_table.shape[1] * cache_ckv.shape[1]
    pos_s = jnp.full((b_s,), past_len, dtype=jnp.int32)
    cache_krt = jnp.transpose(cache_krope, (0, 2, 1))
    new_states = [None]

    def hg_dec_fn(layer, q, lf, k, v, gate, gn):
        mix, new_states[0] = _hg_dec(q, lf, k, v, gate, gn, state_hgrn, layer, new_states[0])
        return mix, new_states[0]

    out_s, st_s, ckv_s, kr_s = _trunk(
        x_sample.reshape(b_s, d), pos_s, b_s, w, hg_dec_fn,
        lambda qcat, kvb: _attn_decode(qcat, kvb, cache_ckv, cache_krt, page_table, w["w_uv"]),
        F32)
    b_p, seq, d = x_prompt.shape
    n_meta = meta_tokens.shape[0]
    seq_len = n_meta + seq
    meta = jnp.broadcast_to(meta_tokens.astype(x_prompt.dtype)[None], (b_p, n_meta, d))
    h_p = jnp.concatenate([meta, x_prompt], axis=1).reshape(b_p * seq_len, d)
    tm_p = _row_tile(seq_len)
    out_p, st_p, ckv_p, kr_p = _trunk(
        h_p, jnp.arange(seq_len, dtype=jnp.int32), tm_p, w,
        lambda layer, q, lf, k, v, gate, gn: _hg_rec(q, lf, k, v, gate, gn, b_p, seq_len),
        lambda qcat, kvb: _attn_prompt(qcat, kvb, w["w_uv"], b_p, seq_len),
        BF16)
    st_p = jnp.stack(st_p)
    y_prompt = out_p.reshape(b_p, seq_len, d)[:, n_meta:]
    ckv_prompt = ckv_p.reshape(b_p, seq_len, -1)
    krope_prompt = kr_p.reshape(b_p, seq_len, -1)

    return (y_prompt, out_s.reshape(b_s, dec, d), st_p, ckv_prompt, krope_prompt,
            st_s[-1].astype(state_hgrn.dtype), ckv_s.reshape(b_s, dec, -1), kr_s.reshape(b_s, dec, -1))
```

```python
import functools
import math

import jax
import jax.numpy as jnp
from jax import lax
from jax.experimental import pallas as pl
from jax.experimental.pallas import tpu as pltpu

F32 = jnp.float32
BF16 = jnp.bfloat16

EPS = 1e-6
LB_FLOOR = 1e-30
ROPE_THETA = 10000.0
MASK_VALUE = -1e30

HG_HEADS = 8
MLA_HEADS = 8
HG_CHUNK = 64
HG_SUB = 16
HG_HEADS_PER_STEP = 4
HG_CHUNK_UNROLL = 4
SUBLANES = 8
LOG2E = 1.4426950408889634
ATTN_TILE = 256
ROPE_PAD = 128
DEC_STATE_BLOCK = 4
VMEM_LIMIT_BYTES = 56 * 1024 * 1024

NT = (((1,), (1,)), ((), ()))
TN = (((0,), (0,)), ((), ()))


def _params(*sem):
    return pltpu.CompilerParams(dimension_semantics=sem, vmem_limit_bytes=VMEM_LIMIT_BYTES)


def _rms(x, g):
    return x * lax.rsqrt(jnp.mean(x * x, axis=-1, keepdims=True) + EPS) * g


def _silu(x):
    h = 0.5 * x
    return h + h * jnp.tanh(h)


def _neg_abs(x):
    bits = lax.bitcast_convert_type(x, jnp.uint32) | jnp.uint32(0x80000000)
    return lax.bitcast_convert_type(bits, F32)


def _dot(a, b):
    return jnp.dot(a, b, preferred_element_type=F32)


def _run_staggered(gens):
    started, nxt = [], 0
    while nxt < len(gens) or started:
        if nxt < len(gens):
            started.insert(0, gens[nxt])
            nxt += 1
        started = [g for g in started if next(g, True) is None]


def _row_tile(seq_len, cap=1024):
    best = None
    for t in range(16, min(seq_len, cap) + 1, 16):
        if seq_len % t == 0:
            best = t
    assert best is not None, seq_len
    return best


def _hg_proj_kernel(h_ref, g_ref, wq_ref, wf_ref, wi_ref, wg_ref, lb_ref,
                    q_ref, lf_ref, k_ref, v_ref, gate_ref, xn_scr):
    xn_scr[...] = _rms(h_ref[...], g_ref[...]).astype(BF16)
    lb = lb_ref[...]
    log_1m_lb = jnp.log1p(-lb)
    log_lb = jnp.log(jnp.maximum(lb, LB_FLOOR))

    def silu_to(ref):
        def store(rows, y):
            ref[rows, :] = _silu(y).astype(ref.dtype)
        return store

    def forget_gate(rows, z):
        t = jnp.exp2(_neg_abs(z * LOG2E))
        t1 = 1.0 + t
        a = log_1m_lb + (jnp.minimum(z, 0.0) - jnp.log(t1))
        u = jnp.exp2(_neg_abs((a - log_lb) * LOG2E))
        lf_ref[rows, :] = jnp.maximum(a, log_lb) + jnp.log(1.0 + u)
        k_ref[rows, :] = ((1.0 - lb) * (jnp.where(z >= 0.0, t, 1.0) / t1)).astype(k_ref.dtype)

    def value(rows, y):
        v_ref[rows, :] = y.astype(v_ref.dtype)

    def item(rows, w_ref, epilogue):
        y = _dot(xn_scr[rows, :], w_ref[...])
        yield
        epilogue(rows, y)

    rows = slice(0, h_ref.shape[0])
    _run_staggered([item(rows, w_ref, ep)
                    for w_ref, ep in ((wq_ref, silu_to(q_ref)), (wf_ref, forget_gate),
                                      (wi_ref, value), (wg_ref, silu_to(gate_ref)))])


def _hg_proj(h, g, wq, wf, wi, wg, lb, tm, act_dtype):
    T, D = h.shape
    N = wq.shape[1]
    row = pl.BlockSpec((tm, D), lambda i: (i, 0))
    vec = lambda n: pl.BlockSpec((1, n), lambda i: (0, 0))
    mat = lambda a: pl.BlockSpec(a.shape, lambda i: (0, 0))
    out = lambda n: pl.BlockSpec((tm, n), lambda i: (i, 0))
    return pl.pallas_call(
        _hg_proj_kernel,
        grid=(T // tm,),
        in_specs=[row, vec(D), mat(wq), mat(wf), mat(wi), mat(wg), vec(N)],
        out_specs=[out(N), out(N), out(N), out(D), out(D)],
        out_shape=[jax.ShapeDtypeStruct((T, N), act_dtype), jax.ShapeDtypeStruct((T, N), F32),
                   jax.ShapeDtypeStruct((T, N), act_dtype), jax.ShapeDtypeStruct((T, D), act_dtype),
                   jax.ShapeDtypeStruct((T, D), act_dtype)],
        scratch_shapes=[pltpu.VMEM((tm, D), BF16)],
        compiler_params=_params("parallel"),
        name="hg_proj",
    )(h, g, wq, wf, wi, wg, lb)


def _hg_rec_kernel(q_ref, lf_ref, k_ref, v_ref, gate_ref, gn_ref, o_ref, s_ref, st_scr, row_scr,
                   *, seq_len, heads):
    st_scr[...] = jnp.zeros_like(st_scr)
    kd = q_ref.shape[-1] // heads
    vd = v_ref.shape[-1] // heads
    ones_kk = jnp.ones((kd, kd), BF16)
    t_idx = lax.broadcasted_iota(jnp.int32, (SUBLANES, 1), 0)

    def head_chunk(g, r0, C, u=0):
        rows = pl.ds(r0, C)
        kl = slice(g * kd, (g + 1) * kd)
        vl = slice(g * vd, (g + 1) * vd)
        n_sub = C // HG_SUB
        lf = lf_ref[0, rows, kl] * LOG2E
        tri = (lax.broadcasted_iota(jnp.int32, (C, C), 0)
               >= lax.broadcasted_iota(jnp.int32, (C, C), 1)).astype(BF16)
        hi = lf.astype(BF16)
        r1 = lf - hi.astype(F32)
        mid = r1.astype(BF16)
        lo = (r1 - mid.astype(F32)).astype(BF16)
        b = _dot(tri, hi) + _dot(tri, mid) + _dot(tri, lo)
        q = q_ref[0, rows, kl].astype(F32)
        k = k_ref[0, rows, kl].astype(F32)
        vb = v_ref[0, rows, vl].astype(BF16)
        vf = vb.astype(F32)
        row_scr[u, g, 0, :C] = b
        row_scr[u, g, 1, :C] = k
        row_scr[u, g, 2, :C] = vf
        b_row = lambda t: row_scr[u, g, 0, t:t + 1, :]
        k_row = lambda t: row_scr[u, g, 1, t:t + 1, :]
        v_row = lambda t: row_scr[u, g, 2, t:t + 1, :]
        yield
        st = st_scr[g]
        o = lax.dot_general((q * jnp.exp2(b)).astype(BF16), st.astype(BF16), NT,
                            preferred_element_type=F32)
        scores = []
        for i in range(1, n_sub):
            lo_r = i * HG_SUB
            anchor = b[lo_r - 1:lo_r]
            qd = (q[lo_r:lo_r + HG_SUB] * jnp.exp2(b[lo_r:lo_r + HG_SUB] - anchor)).astype(BF16)
            kdec = (k[:lo_r] * jnp.exp2(anchor - b[:lo_r])).astype(BF16)
            scores.append(lax.dot_general(qd, kdec, NT, preferred_element_type=F32))
        b_last = b[C - 1:C]
        kdec = (k * jnp.exp2(b_last - b)).astype(BF16)
        st_scr[g] = st * jnp.exp2(b_last) + lax.dot_general(vb, kdec, TN,
                                                            preferred_element_type=F32)
        yield
        off = [None] + [_dot(sc.astype(BF16), vb[:(i + 1) * HG_SUB])
                        for i, sc in enumerate(scores)]
        sums = []
        for i in range(n_sub):
            lo_r, hi_r = i * HG_SUB, (i + 1) * HG_SUB
            b_i, q_i = b[lo_r:hi_r], q[lo_r:hi_r]
            prods = []
            for s in range(HG_SUB):
                for p0 in range((s // SUBLANES) * SUBLANES, HG_SUB, SUBLANES):
                    e = jnp.exp2(b_i[p0:p0 + SUBLANES] - b_row(lo_r + s))
                    if p0 <= s:
                        e = jnp.where(t_idx >= s - p0, e, 0.0)
                    prods.append(q_i[p0:p0 + SUBLANES] * e * k_row(lo_r + s))
            sums.append(_dot(jnp.concatenate(prods, axis=0).astype(BF16), ones_kk))
            yield
        parts = []
        for i in range(n_sub):
            lo_r = i * HG_SUB
            acc = {p0: (jnp.zeros((SUBLANES, vd), F32) if off[i] is None
                        else off[i][p0:p0 + SUBLANES]) for p0 in range(0, HG_SUB, SUBLANES)}
            n = 0
            for s in range(HG_SUB):
                for p0 in range((s // SUBLANES) * SUBLANES, HG_SUB, SUBLANES):
                    acc[p0] = acc[p0] + sums[i][n * SUBLANES:(n + 1) * SUBLANES] * v_row(lo_r + s)
                    n += 1
            parts.extend(acc[p0] for p0 in sorted(acc))
        o = o + jnp.concatenate(parts, axis=0)
        gate = gate_ref[0, rows, vl].astype(F32)
        o_ref[0, rows, vl] = (_rms(o, gn_ref[:, vl]) * gate).astype(o_ref.dtype)

    def chunks(starts, C):
        live = [head_chunk(g, r0, C, u) for u, r0 in enumerate(starts) for g in range(heads)]
        while live:
            live = [gen for gen in live if next(gen, True) is None]

    n_full = seq_len // HG_CHUNK
    tail = seq_len - n_full * HG_CHUNK
    unroll = HG_CHUNK_UNROLL if n_full % HG_CHUNK_UNROLL == 0 else 1

    def body(c, carry):
        base = pl.multiple_of(c * (unroll * HG_CHUNK), unroll * HG_CHUNK)
        chunks([base + u * HG_CHUNK for u in range(unroll)], HG_CHUNK)
        return carry

    lax.fori_loop(0, n_full // unroll, body, 0)
    if tail:
        chunks([n_full * HG_CHUNK], tail)
    for g in range(heads):
        s_ref[0, g] = st_scr[g].T


def _hg_rec(q, lf, k, v, gate, gn, batch, seq_len):
    T, N = q.shape
    D = v.shape[1]
    kd, vd = N // HG_HEADS, D // HG_HEADS
    G = HG_HEADS_PER_STEP
    assert seq_len % HG_SUB == 0 and HG_HEADS % G == 0
    r3 = lambda a: a.reshape(batch, seq_len, a.shape[1])
    seq_k = pl.BlockSpec((1, seq_len, G * kd), lambda b, h: (b, 0, h))
    seq_v = pl.BlockSpec((1, seq_len, G * vd), lambda b, h: (b, 0, h))
    o, s = pl.pallas_call(
        functools.partial(_hg_rec_kernel, seq_len=seq_len, heads=G),
        grid=(batch, HG_HEADS // G),
        in_specs=[seq_k, seq_k, seq_k, seq_v, seq_v,
                  pl.BlockSpec((1, G * vd), lambda b, h: (0, h))],
        out_specs=[seq_v, pl.BlockSpec((1, G, kd, vd), lambda b, h: (b, h, 0, 0))],
        out_shape=[jax.ShapeDtypeStruct((batch, seq_len, D), BF16),
                   jax.ShapeDtypeStruct((batch, HG_HEADS, kd, vd), F32)],
        scratch_shapes=[pltpu.VMEM((G, vd, kd), F32), pltpu.VMEM((HG_CHUNK_UNROLL, G, 3, HG_CHUNK, kd), F32)],
        compiler_params=_params("parallel", "parallel"),
        name="hg_rec",
    )(r3(q), r3(lf), r3(k), r3(v), r3(gate), gn)
    return o.reshape(T, D), s


def _hg_dec_kernel(q_ref, lf_ref, k_ref, v_ref, gate_ref, gn_ref, s0_ref, *rest, out_layer):
    o_ref, s_ref = rest[-2:]
    _, nb, nh, _, vd = s0_ref.shape
    for other in range(s_ref.shape[0]):
        if other != out_layer:
            s_ref[other] = jnp.zeros(s_ref.shape[1:], s_ref.dtype)
    for b in range(nb):
        outs = []
        for h in range(nh):
            f_col = jnp.exp(lf_ref[b, :, h:h + 1])
            k_col = k_ref[b, :, h:h + 1]
            q_col = q_ref[b, :, h:h + 1]
            v_row = v_ref[b, :, h * vd:(h + 1) * vd]
            s_new = f_col * s0_ref[0, b, h] + k_col * v_row
            s_ref[out_layer, b, h] = s_new
            o = jnp.sum(s_new * q_col, axis=0, keepdims=True)
            outs.append(_rms(o, gn_ref[:, h * vd:(h + 1) * vd]))
        o_all = jnp.concatenate(outs, axis=1) * gate_ref[b]
        o_ref[b] = o_all.astype(o_ref.dtype)


def _hg_dec(q, lf, k, v, gate, gn, state_all, layer, new_state_all):
    n_layers, nseq, nh, kd, vd = state_all.shape
    D = nh * vd
    nb = DEC_STATE_BLOCK if nseq % DEC_STATE_BLOCK == 0 else 1
    km = lambda a: jnp.transpose(a.reshape(nseq, nh, kd), (0, 2, 1))
    col = pl.BlockSpec((nb, kd, nh), lambda i: (i, 0, 0))
    rowv = pl.BlockSpec((nb, 1, D), lambda i: (i, 0, 0))
    st = pl.BlockSpec((1, nb, nh, kd, vd), lambda i: (layer, i, 0, 0, 0))
    in_specs = [col, col, col, rowv, rowv, pl.BlockSpec((1, D), lambda i: (0, 0)), st]
    args = [km(q), km(lf), km(k), v.reshape(nseq, 1, D), gate.reshape(nseq, 1, D), gn, state_all]
    if new_state_all is None:
        st_out, out_layer, aliases = pl.BlockSpec((n_layers, nb, nh, kd, vd),
                                                  lambda i: (0, i, 0, 0, 0)), layer, {}
    else:
        in_specs.append(pl.BlockSpec(memory_space=pl.ANY))
        args.append(new_state_all)
        st_out, out_layer, aliases = st, 0, {len(args) - 1: 1}
    o, s = pl.pallas_call(
        functools.partial(_hg_dec_kernel, out_layer=out_layer),
        grid=(nseq // nb,),
        in_specs=in_specs,
        out_specs=[rowv, st_out],
        out_shape=[jax.ShapeDtypeStruct((nseq, 1, D), BF16),
                   jax.ShapeDtypeStruct(state_all.shape, F32)],
        input_output_aliases=aliases,
        compiler_params=_params("parallel"),
        name="hg_dec",
    )(*args)
    return o.reshape(nseq, D), s


def _mlp_kernel(h_ref, mix_ref, wmix_ref, g_ref, wup_ref, wdn_ref, gf_ref, o_ref, xn_scr,
                *, final_norm):
    j = pl.program_id(1)

    @pl.when(j == 0)
    def _():
        h1 = h_ref[...] + _dot(mix_ref[...], wmix_ref[...])
        o_ref[...] = h1
        xn_scr[...] = _rms(h1, g_ref[...]).astype(BF16)

    u = jnp.maximum(_dot(xn_scr[...], wup_ref[...]), 0.0)
    o_ref[...] += _dot((u * u).astype(BF16), wdn_ref[...])

    if final_norm:
        @pl.when(j == pl.num_programs(1) - 1)
        def _():
            o_ref[...] = _rms(o_ref[...], gf_ref[...])


def _mlp(h, mix, wmix, g, wup, wdn, gf, tm, final_norm, keep=None):
    T, D = h.shape
    dff = wup.shape[1]
    tf = min(dff, 1024)
    if keep is None:
        n_tiles = T // tm
        row_in = lambda: pl.BlockSpec((tm, D), lambda i, j: (i, 0))
    else:
        n_seq, seq_len, skip = keep
        tm = _row_tile(seq_len - skip, cap=512)
        per_seq = (seq_len - skip) // tm
        n_tiles = n_seq * per_seq
        row_in = lambda: pl.BlockSpec(
            (pl.Element(tm), pl.Element(D)),
            lambda i, j: (pl.multiple_of((i // per_seq) * seq_len + skip + (i % per_seq) * tm,
                                         math.gcd(seq_len, skip, tm)), 0))
    vec = lambda: pl.BlockSpec((1, D), lambda i, j: (0, 0))
    return pl.pallas_call(
        functools.partial(_mlp_kernel, final_norm=final_norm),
        grid=(n_tiles, dff // tf),
        in_specs=[row_in(), row_in(), pl.BlockSpec((D, D), lambda i, j: (0, 0)), vec(),
                  pl.BlockSpec((D, tf), lambda i, j: (0, j)),
                  pl.BlockSpec((tf, D), lambda i, j: (j, 0)), vec()],
        out_specs=pl.BlockSpec((tm, D), lambda i, j: (i, 0)),
        out_shape=jax.ShapeDtypeStruct((n_tiles * tm, D), F32),
        scratch_shapes=[pltpu.VMEM((tm, D), BF16)],
        compiler_params=_params("parallel", "arbitrary"),
        name="mlp",
    )(h, mix, wmix, g, wup, wdn, gf)


def _kv_kernel(h_ref, g_ref, wdkv_ref, gkv_ref, wkr_ref, wkrs_ref, cos_ref, sin_ref,
               ckv_ref, kr_ref, kvb_ref):
    hn = _rms(h_ref[...], g_ref[...]).astype(BF16)
    ckv = _rms(_dot(hn, wdkv_ref[...]), gkv_ref[...])
    kr = _dot(hn, wkr_ref[...]) * cos_ref[...] + _dot(hn, wkrs_ref[...]) * sin_ref[...]
    ckv_ref[...] = ckv
    kr_ref[...] = kr
    r, rd = ckv.shape[1], kr.shape[1]
    kvb_ref[:, :r] = ckv.astype(BF16)
    kvb_ref[:, r:r + rd] = kr.astype(BF16)
    kvb_ref[:, r + rd:] = jnp.zeros((kr.shape[0], kvb_ref.shape[1] - r - rd), BF16)


def _kv(h, g, wdkv, gkv, wkr, wkrs, cos, sin, tm):
    T, D = h.shape
    r, rd = wdkv.shape[1], wkr.shape[1]
    npos = cos.shape[0] // tm
    row = lambda n: pl.BlockSpec((tm, n), lambda i: (i, 0))
    vec = lambda n: pl.BlockSpec((1, n), lambda i: (0, 0))
    mat = lambda a: pl.BlockSpec(a.shape, lambda i: (0, 0))
    tab = pl.BlockSpec((tm, rd), lambda i: (i % npos, 0))
    return pl.pallas_call(
        _kv_kernel,
        grid=(T // tm,),
        in_specs=[row(D), vec(D), mat(wdkv), vec(r), mat(wkr), mat(wkrs), tab, tab],
        out_specs=[row(r), row(rd), row(r + ROPE_PAD)],
        out_shape=[jax.ShapeDtypeStruct((T, r), F32), jax.ShapeDtypeStruct((T, rd), F32),
                   jax.ShapeDtypeStruct((T, r + ROPE_PAD), BF16)],
        compiler_params=_params("parallel"),
        name="shared_kv",
    )(h, g, wdkv, gkv, wkr, wkrs, cos, sin)


def _q_kernel(h_ref, g_ref, wdq_ref, gq_ref, wn_ref, wr_ref, wrs_ref, wuk_ref, cos_ref, sin_ref,
              o_ref, *, scale):
    xn = _rms(h_ref[...], g_ref[...]).astype(BF16)
    cq = _rms(_dot(xn, wdq_ref[...]), gq_ref[...]).astype(BF16)
    qn = (_dot(cq, wn_ref[...]) * scale).astype(BF16)
    qr = ((_dot(cq, wr_ref[...]) * cos_ref[...] + _dot(cq, wrs_ref[...]) * sin_ref[...])
          * scale).astype(BF16)
    nh, nd, r = wuk_ref.shape
    rd = qr.shape[1] // nh
    for h in range(nh):
        o_ref[h, :, :r] = _dot(qn[:, h * nd:(h + 1) * nd], wuk_ref[h]).astype(BF16)
        o_ref[h, :, r:r + rd] = qr[:, h * rd:(h + 1) * rd]
        o_ref[h, :, r + rd:] = jnp.zeros((qr.shape[0], o_ref.shape[2] - r - rd), BF16)


def _mla_q(h, g, wdq, gq, wn, wr, wrs, wuk, cos, sin, tm, scale):
    T, D = h.shape
    nh, _, r = wuk.shape
    npos = cos.shape[0] // tm
    vec = lambda n: pl.BlockSpec((1, n), lambda i: (0, 0))
    mat = lambda a: pl.BlockSpec(a.shape, lambda i: (0,) * a.ndim)
    tab = pl.BlockSpec((tm, cos.shape[1]), lambda i: (i % npos, 0))
    return pl.pallas_call(
        functools.partial(_q_kernel, scale=scale),
        grid=(T // tm,),
        in_specs=[pl.BlockSpec((tm, D), lambda i: (i, 0)), vec(D), mat(wdq), vec(wdq.shape[1]),
                  mat(wn), mat(wr), mat(wrs), mat(wuk), tab, tab],
        out_specs=pl.BlockSpec((nh, tm, r + ROPE_PAD), lambda i: (0, i, 0)),
        out_shape=jax.ShapeDtypeStruct((nh, T, r + ROPE_PAD), BF16),
        compiler_params=_params("parallel"),
        name="mla_q",
    )(h, g, wdq, gq, wn, wr, wrs, wuk, cos, sin)


def _attn_kernel(q_ref, kv_ref, kvt_ref, wuv_ref, o_ref, m_scr, l_scr, acc_scr, *, seq_len, r):
    i = pl.program_id(1)
    nh = q_ref.shape[0]
    tq = ATTN_TILE
    n_full = seq_len // tq
    tail = seq_len - n_full * tq
    vd = wuv_ref.shape[1] // nh

    def run_full():
        m_scr[...] = jnp.full_like(m_scr, MASK_VALUE)
        l_scr[...] = jnp.zeros_like(l_scr)
        acc_scr[...] = jnp.zeros_like(acc_scr)

        def scores(h, j):
            kv = kv_ref[0, pl.ds(pl.multiple_of(j * tq, tq), tq), :]
            return lax.dot_general(kv, q_ref[h, 0], NT, preferred_element_type=F32)

        def head_block(h, j, s, s_next, mask):
            if s_next is not None:
                s_next[h] = scores(h, j + 1)
            yield
            if mask is not None:
                s = jnp.where(mask, s, MASK_VALUE)
            m = m_scr[h]
            m_new = jnp.maximum(m, jnp.max(s, axis=0, keepdims=True))
            alpha = jnp.exp2(m - m_new)
            p = jnp.exp2(s - m_new)
            l_scr[h] = alpha * l_scr[h] + jnp.sum(p, axis=0, keepdims=True)
            m_scr[h] = m_new
            kvt = kvt_ref[0, :, pl.ds(pl.multiple_of(j * tq, tq), tq)]
            pv = _dot(kvt, p.astype(BF16))
            yield
            acc_scr[h] = alpha * acc_scr[h] + pv

        def body(j, s_cur):
            s_next = [None] * nh
            _run_staggered([head_block(h, j, s_cur[h], s_next, None) for h in range(nh)])
            return tuple(s_next)

        s_diag = lax.fori_loop(0, i, body, tuple(scores(h, 0) for h in range(nh)))
        key_loc = lax.broadcasted_iota(jnp.int32, (tq, tq), 0)
        qry_loc = lax.broadcasted_iota(jnp.int32, (tq, tq), 1)
        _run_staggered([head_block(h, i, s_diag[h], None, key_loc <= qry_loc) for h in range(nh)])
        for h in range(nh):
            o_t = (acc_scr[h] * (1.0 / l_scr[h])).astype(BF16)
            o_ref[0, :, h * vd:(h + 1) * vd] = lax.dot_general(
                o_t, wuv_ref[:, h * vd:(h + 1) * vd], TN,
                preferred_element_type=F32).astype(o_ref.dtype)

    def run_tail(rows):
        m_rows = nh * rows
        q = q_ref[:, 0, :rows, :].reshape(m_rows, q_ref.shape[-1])

        def step(carry, kv, mask):
            m, l, acc = carry
            s = lax.dot_general(q, kv, NT, preferred_element_type=F32)
            if mask is not None:
                s = jnp.where(mask, s, MASK_VALUE)
            m_new = jnp.maximum(m, jnp.max(s, axis=-1, keepdims=True))
            alpha = jnp.exp2(m - m_new)
            p = jnp.exp2(s - m_new)
            l = alpha * l + jnp.sum(p, axis=-1, keepdims=True)
            acc = alpha * acc + _dot(p.astype(BF16), kv[:, :r])
            return m_new, l, acc

        init = (jnp.full((m_rows, 1), MASK_VALUE, F32), jnp.zeros((m_rows, 1), F32),
                jnp.zeros((m_rows, r), F32))
        carry = lax.fori_loop(
            0, n_full,
            lambda j, c: step(c, kv_ref[0, pl.ds(pl.multiple_of(j * tq, tq), tq), :], None),
            init)
        t_loc = lax.broadcasted_iota(jnp.int32, (m_rows, rows), 0) % rows
        s_loc = lax.broadcasted_iota(jnp.int32, (m_rows, rows), 1)
        _, l, acc = step(carry, kv_ref[0, n_full * tq:n_full * tq + rows, :], s_loc <= t_loc)
        o_lat = (acc / l).astype(BF16)
        for h in range(nh):
            o_ref[0, :rows, h * vd:(h + 1) * vd] = _dot(
                o_lat[h * rows:(h + 1) * rows], wuv_ref[:, h * vd:(h + 1) * vd]).astype(o_ref.dtype)

    if n_full:
        pl.when(i < n_full)(run_full)
    if tail:
        pl.when(i == n_full)(lambda: run_tail(tail))


def _attn_prompt(qcat, kvb, wuv, batch, seq_len):
    nh, T, w = qcat.shape
    r = w - ROPE_PAD
    D = wuv.shape[1]
    tq = ATTN_TILE
    nq = pl.cdiv(seq_len, tq)
    kv3 = kvb.reshape(batch, seq_len, w)
    kvt = jnp.swapaxes(kv3[:, :, :r], 1, 2)
    out = pl.pallas_call(
        functools.partial(_attn_kernel, seq_len=seq_len, r=r),
        grid=(batch, nq),
        in_specs=[pl.BlockSpec((nh, 1, tq, w), lambda b, i: (0, b, i, 0)),
                  pl.BlockSpec((1, seq_len, w), lambda b, i: (b, 0, 0)),
                  pl.BlockSpec((1, r, seq_len), lambda b, i: (b, 0, 0)),
                  pl.BlockSpec(wuv.shape, lambda b, i: (0, 0))],
        out_specs=pl.BlockSpec((1, tq, D), lambda b, i: (b, i, 0)),
        out_shape=jax.ShapeDtypeStruct((batch, seq_len, D), BF16),
        scratch_shapes=[pltpu.VMEM((nh, 1, tq), F32), pltpu.VMEM((nh, 1, tq), F32),
                        pltpu.VMEM((nh, r, tq), F32)],
        compiler_params=_params("parallel", "parallel"),
        name="attn_prompt",
    )(qcat.reshape(nh, batch, seq_len, w), kv3, kvt, wuv)
    return out.reshape(T, D)


def _attn_dec_kernel(pt_ref, q_ref, kvn_ref, wuv_ref, ckv_hbm, krt_hbm, o_ref,
                     ckv_buf, krt_buf, kv_scr, krt_scr, sem, *, r, rd):
    b = pl.program_id(0)
    n_pg, _, page = krt_buf.shape[1:]

    def page_copies(seq, slot, p):
        idx = pt_ref[seq, p]
        rows = pl.ds(pl.multiple_of(p * page, page), page)
        return (pltpu.make_async_copy(ckv_hbm.at[idx], ckv_buf.at[slot, rows], sem.at[0, slot]),
                pltpu.make_async_copy(krt_hbm.at[idx], krt_buf.at[slot, p], sem.at[1, slot]))

    def for_pages(seq, slot, act):
        def body(p, carry):
            for c in page_copies(seq, slot, p):
                act(c)
            return carry
        lax.fori_loop(0, n_pg, body, 0)

    slot = b % 2

    @pl.when(b == 0)
    def _():
        for_pages(0, 0, lambda c: c.start())

    @pl.when(b + 1 < pl.num_programs(0))
    def _():
        for_pages(b + 1, 1 - slot, lambda c: c.start())

    for_pages(b, slot, lambda c: c.wait())

    q = q_ref[0]
    nh = q.shape[0]
    half = (n_pg // 2) * page
    s_parts = []
    for side in range(2):
        lanes = slice(side * r, (side + 1) * r)
        keys = slice(side * half, (side + 1) * half)
        kv_scr[:, lanes] = ckv_buf[slot, keys, :].astype(BF16)
        for p in range(side * (n_pg // 2), (side + 1) * (n_pg // 2)):
            krt_scr[:, p * page:(p + 1) * page] = krt_buf[slot, p].astype(BF16)
        s_parts.append(lax.dot_general(q[:, :r], kv_scr[:, lanes], NT, preferred_element_type=F32)
                       + _dot(q[:, r:r + rd], krt_scr[:, keys]))
    kvn = kvn_ref[0].astype(F32)
    s_new = jnp.sum(q.astype(F32) * kvn, axis=-1, keepdims=True)
    m = s_new
    for s in s_parts:
        m = jnp.maximum(m, jnp.max(s, axis=-1, keepdims=True))
    p_new = jnp.exp2(s_new - m)
    p_parts = [jnp.exp2(s - m) for s in s_parts]
    l = p_new + sum(jnp.sum(p, axis=-1, keepdims=True) for p in p_parts)
    pv = _dot(jnp.concatenate(p_parts, axis=0).astype(BF16), kv_scr[...])
    o_lat = (pv[:nh, :r] + pv[nh:, r:] + p_new * kvn[:, :r]) / l
    full = _dot(o_lat.astype(BF16), wuv_ref[...])
    vd = full.shape[1] // nh
    head_of_lane = lax.broadcasted_iota(jnp.int32, full.shape, 1) // vd
    head_of_row = lax.broadcasted_iota(jnp.int32, full.shape, 0)
    o_ref[0] = jnp.sum(jnp.where(head_of_lane == head_of_row, full, 0.0), axis=0,
                       keepdims=True).astype(o_ref.dtype)


def _attn_decode(qcat, kvb_new, cache_ckv, cache_krt, page_table, wuv):
    nh, nseq, w = qcat.shape
    r = w - ROPE_PAD
    rd, page = cache_krt.shape[1:]
    n_pg = page_table.shape[1]
    assert n_pg % 2 == 0
    D = wuv.shape[1]
    q = jnp.transpose(qcat, (1, 0, 2))
    grid_spec = pltpu.PrefetchScalarGridSpec(
        num_scalar_prefetch=1,
        grid=(nseq,),
        in_specs=[pl.BlockSpec((1, nh, w), lambda b, pt: (b, 0, 0)),
                  pl.BlockSpec((1, 1, w), lambda b, pt: (b, 0, 0)),
                  pl.BlockSpec(wuv.shape, lambda b, pt: (0, 0)),
                  pl.BlockSpec(memory_space=pl.ANY),
                  pl.BlockSpec(memory_space=pl.ANY)],
        out_specs=pl.BlockSpec((1, 1, D), lambda b, pt: (b, 0, 0)),
        scratch_shapes=[pltpu.VMEM((2, n_pg * page, r), F32), pltpu.VMEM((2, n_pg, rd, page), F32),
                        pltpu.VMEM((n_pg // 2 * page, 2 * r), BF16), pltpu.VMEM((rd, n_pg * page), BF16),
                        pltpu.SemaphoreType.DMA((2, 2))],
    )
    out = pl.pallas_call(
        functools.partial(_attn_dec_kernel, r=r, rd=rd),
        grid_spec=grid_spec,
        out_shape=jax.ShapeDtypeStruct((nseq, 1, D), BF16),
        compiler_params=_params("arbitrary"),
        name="attn_decode",
    )(page_table, q, kvb_new.reshape(nseq, 1, w), wuv, cache_ckv, cache_krt)
    return out.reshape(nseq, D)


def _rope_tables(pos, rope_dim):
    half = rope_dim // 2
    inv_freq = ROPE_THETA ** (-jnp.arange(half, dtype=F32) / half)
    ang = pos.astype(F32)[:, None] * inv_freq[None, :]
    cos, sin = jnp.cos(ang), jnp.sin(ang)
    return jnp.concatenate([cos, cos], axis=1), jnp.concatenate([-sin, sin], axis=1)


def _swap_halves(w, rope_dim):
    lead = w.shape[0]
    w3 = w.reshape(lead, -1, rope_dim)
    half = rope_dim // 2
    return jnp.concatenate([w3[..., half:], w3[..., :half]], axis=-1).reshape(w.shape)


def _prep_weights(p):
    bf = lambda a: a.astype(BF16)
    depth = p["norm_mix"].shape[0]
    n_a = p["hg_wq"].shape[0]
    r, nh, nd = p["w_uk"].shape
    rope_dim = p["w_kr"].shape[1]
    lbs = jax.nn.softmax(p["hg_lower_bounds"].astype(F32), axis=0)
    lbs = jnp.cumsum(lbs, axis=0) - lbs[0]
    w_uq = p["w_uq"]
    nb = w_uq.shape[0]
    wn = w_uq[..., :nd].reshape(nb, w_uq.shape[1], nh * nd)
    wr = w_uq[..., nd:].reshape(nb, w_uq.shape[1], nh * rope_dim)
    return dict(
        depth=depth, n_a=n_a, rope_dim=rope_dim, nope_dim=nd,
        norm_mix=p["norm_mix"][:, None, :], norm_mlp=p["norm_mlp"][:, None, :],
        norm_final=p["norm_final"][None, :],
        hg_wq=bf(p["hg_wq"]), hg_wf=bf(p["hg_wf"]), hg_wi=bf(p["hg_wi"]), hg_wg=bf(p["hg_wg"]),
        hg_wo=bf(p["hg_wo"]), hg_gnorm=p["hg_gnorm"][:, None, :], lbs=lbs[:, None, :],
        norm_kv=p["norm_kv"][None, :], w_dkv=bf(p["w_dkv"]), kv_norm=p["kv_norm"][None, :],
        w_kr=bf(p["w_kr"]), w_kr_sw=bf(_swap_halves(p["w_kr"], rope_dim)),
        w_ukt=bf(jnp.transpose(p["w_uk"], (1, 2, 0))),
        w_uv=bf(p["w_uv"].reshape(r, -1)),
        w_dq=bf(p["w_dq"]), q_norm=p["q_norm"][:, None, :],
        w_uq_n=bf(wn), w_uq_r=bf(wr),
        w_uq_rs=bf(jnp.stack([_swap_halves(wr[j], rope_dim) for j in range(nb)])),
        w_o=bf(p["w_o"]), w_up=bf(p["w_up"]), w_down=bf(p["w_down"]),
    )


def _trunk(h, pos_tile, tm, w, hg_rec_fn, attend_fn, act_dtype, out_rows=None):
    rope_dim = w["rope_dim"]
    cos, sin = _rope_tables(pos_tile, rope_dim)
    cos_q, sin_q = jnp.tile(cos, (1, MLA_HEADS)), jnp.tile(sin, (1, MLA_HEADS))
    scale = float((w["nope_dim"] + rope_dim) ** -0.5) * LOG2E
    depth, n_a = w["depth"], w["n_a"]
    states = []
    ckv = krope = kvb = None
    for layer in range(depth):
        last = layer == depth - 1
        if layer < n_a:
            q, lf, k, v, gate = _hg_proj(h, w["norm_mix"][layer], w["hg_wq"][layer], w["hg_wf"][layer],
                                         w["hg_wi"][layer], w["hg_wg"][layer], w["lbs"][layer],
                                         tm, act_dtype)
            mix, s_new = hg_rec_fn(layer, q, lf, k, v, gate, w["hg_gnorm"][layer])
            states.append(s_new)
            w_mix = w["hg_wo"][layer]
        else:
            j = layer - n_a
            qcat = _mla_q(h, w["norm_mix"][layer], w["w_dq"][j], w["q_norm"][j], w["w_uq_n"][j],
                          w["w_uq_r"][j], w["w_uq_rs"][j], w["w_ukt"], cos_q, sin_q, tm, scale)
            mix = attend_fn(qcat, kvb)
            w_mix = w["w_o"][j]
        h = _mlp(h, mix, w_mix, w["norm_mlp"][layer], w["w_up"][layer], w["w_down"][layer],
                 w["norm_final"], tm, final_norm=last, keep=out_rows if last else None)
        if layer == n_a - 1:
            ckv, krope, kvb = _kv(h, w["norm_kv"], w["w_dkv"], w["kv_norm"], w["w_kr"], w["w_kr_sw"],
                                  cos, sin, tm)
    return h, states, ckv, krope


def kernel(x_prompt, x_sample, state_hgrn, cache_ckv, cache_krope, page_table, meta_tokens, norm_mix, norm_mlp, norm_final, hg_wq, hg_wf, hg_wi, hg_wg, hg_gnorm, hg_wo, hg_lower_bounds, norm_kv, w_dkv, kv_norm, w_kr, w_uk, w_uv, w_dq, q_norm, w_uq, w_o, w_up, w_down):
    w = _prep_weights(dict(
        norm_mix=norm_mix, norm_mlp=norm_mlp, norm_final=norm_final, hg_wq=hg_wq, hg_wf=hg_wf,
        hg_wi=hg_wi, hg_wg=hg_wg, hg_gnorm=hg_gnorm, hg_wo=hg_wo, hg_lower_bounds=hg_lower_bounds,
        norm_kv=norm_kv, w_dkv=w_dkv, kv_norm=kv_norm, w_kr=w_kr, w_uk=w_uk, w_uv=w_uv, w_dq=w_dq,
        q_norm=q_norm, w_uq=w_uq, w_o=w_o, w_up=w_up, w_down=w_down))

    b_s, dec, d = x_sample.shape
    assert dec == 1
    past_len = page_table.shape[1] * cache_ckv.shape[1]
    pos_s = jnp.full((b_s,), past_len, dtype=jnp.int32)
    cache_krt = jnp.transpose(cache_krope, (0, 2, 1))
    new_states = [None]

    def hg_dec_fn(layer, q, lf, k, v, gate, gn):
        mix, new_states[0] = _hg_dec(q, lf, k, v, gate, gn, state_hgrn, layer, new_states[0])
        return mix, new_states[0]

    out_s, st_s, ckv_s, kr_s = _trunk(
        x_sample.reshape(b_s, d), pos_s, b_s, w, hg_dec_fn,
        lambda qcat, kvb: _attn_decode(qcat, kvb, cache_ckv, cache_krt, page_table, w["w_uv"]),
        F32)
    b_p, seq, d = x_prompt.shape
    n_meta = meta_tokens.shape[0]
    seq_len = n_meta + seq
    meta = jnp.broadcast_to(meta_tokens.astype(x_prompt.dtype)[None], (b_p, n_meta, d))
    h_p = jnp.concatenate([meta, x_prompt], axis=1).reshape(b_p * seq_len, d)
    tm_p = _row_tile(seq_len)
    out_p, st_p, ckv_p, kr_p = _trunk(
        h_p, jnp.arange(seq_len, dtype=jnp.int32), tm_p, w,
        lambda layer, q, lf, k, v, gate, gn: _hg_rec(q, lf, k, v, gate, gn, b_p, seq_len),
        lambda qcat, kvb: _attn_prompt(qcat, kvb, w["w_uv"], b_p, seq_len),
        BF16, out_rows=(b_p, seq_len, n_meta))
    st_p = jnp.stack(st_p)
    y_prompt = out_p.reshape(b_p, seq, d)
    ckv_prompt = ckv_p.reshape(b_p, seq_len, -1)
    krope_prompt = kr_p.reshape(b_p, seq_len, -1)

    return (y_prompt, out_s.reshape(b_s, dec, d), st_p, ckv_prompt, krope_prompt,
            st_s[-1].astype(state_hgrn.dtype), ckv_s.reshape(b_s, dec, -1), kr_s.reshape(b_s, dec, -1))
```

```python
import functools
import math

import jax
import jax.numpy as jnp
from jax import lax
from jax.experimental import pallas as pl
from jax.experimental.pallas import tpu as pltpu

F32 = jnp.float32
BF16 = jnp.bfloat16

EPS = 1e-6
LB_FLOOR = 1e-30
ROPE_THETA = 10000.0
MASK_VALUE = -1e30

HG_HEADS = 8
MLA_HEADS = 8
HG_CHUNK = 64
HG_SUB = 16
HG_HEADS_PER_STEP = 4
HG_CHUNK_UNROLL = 4
SUBLANES = 8
LOG2E = 1.4426950408889634
ATTN_TILE = 256
ROPE_PAD = 128
DEC_STATE_BLOCK = 4
MLP_FF_BLOCK = 2048
PROJ_COL_BLOCK = 256
VMEM_LIMIT_BYTES = 56 * 1024 * 1024

NT = (((1,), (1,)), ((), ()))
TN = (((0,), (0,)), ((), ()))


def _params(*sem):
    return pltpu.CompilerParams(dimension_semantics=sem, vmem_limit_bytes=VMEM_LIMIT_BYTES)


def _rms(x, g):
    return x * lax.rsqrt(jnp.mean(x * x, axis=-1, keepdims=True) + EPS) * g


def _silu(x):
    h = 0.5 * x
    return h + h * jnp.tanh(h)


def _neg_abs(x):
    bits = lax.bitcast_convert_type(x, jnp.uint32) | jnp.uint32(0x80000000)
    return lax.bitcast_convert_type(bits, F32)


def _dot(a, b):
    return jnp.dot(a, b, preferred_element_type=F32)


def _swap_rope_halves(y, rope_dim):
    n = y.shape[1]
    half = rope_dim // 2
    lane = lax.broadcasted_iota(jnp.int32, y.shape, 1)
    return jnp.where(lane % rope_dim < half, pltpu.roll(y, n - half, 1), pltpu.roll(y, half, 1))


def _run_staggered(gens):
    started, nxt = [], 0
    while nxt < len(gens) or started:
        if nxt < len(gens):
            started.insert(0, gens[nxt])
            nxt += 1
        started = [g for g in started if next(g, True) is None]


def _row_tile(seq_len, cap=1024):
    best = None
    for t in range(16, min(seq_len, cap) + 1, 16):
        if seq_len % t == 0:
            best = t
    assert best is not None, seq_len
    return best


def _hg_proj_kernel(h_ref, g_ref, wq_ref, wf_ref, wi_ref, wg_ref, lb_ref,
                    q_ref, lf_ref, k_ref, v_ref, gate_ref, xn_scr):
    xn_scr[...] = _rms(h_ref[...], g_ref[...]).astype(BF16)
    lb = lb_ref[...]
    log_1m_lb = jnp.log1p(-lb)
    log_lb = jnp.log(jnp.maximum(lb, LB_FLOOR))

    def silu_to(ref):
        def store(cols, y):
            ref[:, cols] = _silu(y).astype(ref.dtype)
        return store

    def forget_gate(cols, z):
        t = jnp.exp2(_neg_abs(z * LOG2E))
        t1 = 1.0 + t
        a = log_1m_lb[:, cols] + (jnp.minimum(z, 0.0) - jnp.log(t1))
        u = jnp.exp2(_neg_abs((a - log_lb[:, cols]) * LOG2E))
        lf_ref[:, cols] = jnp.maximum(a, log_lb[:, cols]) + jnp.log(1.0 + u)
        k_ref[:, cols] = ((1.0 - lb[:, cols])
                          * (jnp.where(z >= 0.0, t, 1.0) / t1)).astype(k_ref.dtype)

    def value(cols, y):
        v_ref[:, cols] = y.astype(v_ref.dtype)

    def item(cols, w_ref, epilogue):
        y = _dot(xn_scr[...], w_ref[:, cols])
        yield
        epilogue(cols, y)

    n = wq_ref.shape[1]
    cb = min(n, PROJ_COL_BLOCK)
    _run_staggered([item(slice(c, c + cb), w_ref, ep)
                    for c in range(0, n, cb)
                    for w_ref, ep in ((wf_ref, forget_gate), (wq_ref, silu_to(q_ref)),
                                      (wi_ref, value), (wg_ref, silu_to(gate_ref)))])


def _hg_proj(h, g, wq, wf, wi, wg, lb, tm, act_dtype):
    T, D = h.shape
    N = wq.shape[1]
    row = pl.BlockSpec((tm, D), lambda i: (i, 0))
    vec = lambda n: pl.BlockSpec((1, n), lambda i: (0, 0))
    mat = lambda a: pl.BlockSpec(a.shape, lambda i: (0, 0))
    out = lambda n: pl.BlockSpec((tm, n), lambda i: (i, 0))
    return pl.pallas_call(
        _hg_proj_kernel,
        grid=(T // tm,),
        in_specs=[row, vec(D), mat(wq), mat(wf), mat(wi), mat(wg), vec(N)],
        out_specs=[out(N), out(N), out(N), out(D), out(D)],
        out_shape=[jax.ShapeDtypeStruct((T, N), act_dtype), jax.ShapeDtypeStruct((T, N), F32),
                   jax.ShapeDtypeStruct((T, N), act_dtype), jax.ShapeDtypeStruct((T, D), act_dtype),
                   jax.ShapeDtypeStruct((T, D), act_dtype)],
        scratch_shapes=[pltpu.VMEM((tm, D), BF16)],
        compiler_params=_params("parallel"),
        name="hg_proj",
    )(h, g, wq, wf, wi, wg, lb)


def _hg_rec_kernel(q_ref, lf_ref, k_ref, v_ref, gate_ref, gn_ref, o_ref, s_ref, st_scr, row_scr,
                   *, seq_len, heads):
    st_scr[...] = jnp.zeros_like(st_scr)
    kd = q_ref.shape[-1] // heads
    vd = v_ref.shape[-1] // heads
    ones_kk = jnp.ones((kd, kd), BF16)
    t_idx = lax.broadcasted_iota(jnp.int32, (SUBLANES, 1), 0)

    def head_chunk(g, r0, C, u=0):
        rows = pl.ds(r0, C)
        kl = slice(g * kd, (g + 1) * kd)
        vl = slice(g * vd, (g + 1) * vd)
        n_sub = C // HG_SUB
        lf = lf_ref[0, rows, kl] * LOG2E
        tri = (lax.broadcasted_iota(jnp.int32, (C, C), 0)
               >= lax.broadcasted_iota(jnp.int32, (C, C), 1)).astype(BF16)
        hi = lf.astype(BF16)
        r1 = lf - hi.astype(F32)
        mid = r1.astype(BF16)
        lo = (r1 - mid.astype(F32)).astype(BF16)
        b = _dot(tri, hi) + _dot(tri, mid) + _dot(tri, lo)
        q = q_ref[0, rows, kl].astype(F32)
        k = k_ref[0, rows, kl].astype(F32)
        vb = v_ref[0, rows, vl].astype(BF16)
        vf = vb.astype(F32)
        row_scr[u, g, 0, :C] = b
        row_scr[u, g, 1, :C] = k
        row_scr[u, g, 2, :C] = vf
        b_row = lambda t: row_scr[u, g, 0, t:t + 1, :]
        k_row = lambda t: row_scr[u, g, 1, t:t + 1, :]
        v_row = lambda t: row_scr[u, g, 2, t:t + 1, :]
        yield
        st = st_scr[g]
        o = lax.dot_general((q * jnp.exp2(b)).astype(BF16), st.astype(BF16), NT,
                            preferred_element_type=F32)
        scores = []
        for i in range(1, n_sub):
            lo_r = i * HG_SUB
            anchor = b[lo_r - 1:lo_r]
            qd = (q[lo_r:lo_r + HG_SUB] * jnp.exp2(b[lo_r:lo_r + HG_SUB] - anchor)).astype(BF16)
            kdec = (k[:lo_r] * jnp.exp2(anchor - b[:lo_r])).astype(BF16)
            scores.append(lax.dot_general(qd, kdec, NT, preferred_element_type=F32))
        b_last = b[C - 1:C]
        kdec = (k * jnp.exp2(b_last - b)).astype(BF16)
        st_scr[g] = st * jnp.exp2(b_last) + lax.dot_general(vb, kdec, TN,
                                                            preferred_element_type=F32)
        yield
        off = [None] + [_dot(sc.astype(BF16), vb[:(i + 1) * HG_SUB])
                        for i, sc in enumerate(scores)]
        sums = []
        for i in range(n_sub):
            lo_r, hi_r = i * HG_SUB, (i + 1) * HG_SUB
            b_i, q_i = b[lo_r:hi_r], q[lo_r:hi_r]
            prods = []
            for s in range(HG_SUB):
                for p0 in range((s // SUBLANES) * SUBLANES, HG_SUB, SUBLANES):
                    e = jnp.exp2(b_i[p0:p0 + SUBLANES] - b_row(lo_r + s))
                    if p0 <= s:
                        e = jnp.where(t_idx >= s - p0, e, 0.0)
                    prods.append(q_i[p0:p0 + SUBLANES] * e * k_row(lo_r + s))
            sums.append(_dot(jnp.concatenate(prods, axis=0).astype(BF16), ones_kk))
            yield
        parts = []
        for i in range(n_sub):
            lo_r = i * HG_SUB
            acc = {p0: (jnp.zeros((SUBLANES, vd), F32) if off[i] is None
                        else off[i][p0:p0 + SUBLANES]) for p0 in range(0, HG_SUB, SUBLANES)}
            n = 0
            for s in range(HG_SUB):
                for p0 in range((s // SUBLANES) * SUBLANES, HG_SUB, SUBLANES):
                    acc[p0] = acc[p0] + sums[i][n * SUBLANES:(n + 1) * SUBLANES] * v_row(lo_r + s)
                    n += 1
            parts.extend(acc[p0] for p0 in sorted(acc))
        o = o + jnp.concatenate(parts, axis=0)
        gate = gate_ref[0, rows, vl].astype(F32)
        o_ref[0, rows, vl] = (_rms(o, gn_ref[:, vl]) * gate).astype(o_ref.dtype)

    def chunks(starts, C):
        live = [head_chunk(g, r0, C, u) for u, r0 in enumerate(starts) for g in range(heads)]
        while live:
            live = [gen for gen in live if next(gen, True) is None]

    n_full = seq_len // HG_CHUNK
    tail = seq_len - n_full * HG_CHUNK
    unroll = HG_CHUNK_UNROLL if n_full % HG_CHUNK_UNROLL == 0 else 1

    def body(c, carry):
        base = pl.multiple_of(c * (unroll * HG_CHUNK), unroll * HG_CHUNK)
        chunks([base + u * HG_CHUNK for u in range(unroll)], HG_CHUNK)
        return carry

    lax.fori_loop(0, n_full // unroll, body, 0)
    if tail:
        chunks([n_full * HG_CHUNK], tail)
    for g in range(heads):
        s_ref[0, g] = st_scr[g].T


def _hg_rec(q, lf, k, v, gate, gn, batch, seq_len):
    T, N = q.shape
    D = v.shape[1]
    kd, vd = N // HG_HEADS, D // HG_HEADS
    G = HG_HEADS_PER_STEP
    assert seq_len % HG_SUB == 0 and HG_HEADS % G == 0
    r3 = lambda a: a.reshape(batch, seq_len, a.shape[1])
    seq_k = pl.BlockSpec((1, seq_len, G * kd), lambda b, h: (b, 0, h))
    seq_v = pl.BlockSpec((1, seq_len, G * vd), lambda b, h: (b, 0, h))
    o, s = pl.pallas_call(
        functools.partial(_hg_rec_kernel, seq_len=seq_len, heads=G),
        grid=(batch, HG_HEADS // G),
        in_specs=[seq_k, seq_k, seq_k, seq_v, seq_v,
                  pl.BlockSpec((1, G * vd), lambda b, h: (0, h))],
        out_specs=[seq_v, pl.BlockSpec((1, G, kd, vd), lambda b, h: (b, h, 0, 0))],
        out_shape=[jax.ShapeDtypeStruct((batch, seq_len, D), BF16),
                   jax.ShapeDtypeStruct((batch, HG_HEADS, kd, vd), F32)],
        scratch_shapes=[pltpu.VMEM((G, vd, kd), F32), pltpu.VMEM((HG_CHUNK_UNROLL, G, 3, HG_CHUNK, kd), F32)],
        compiler_params=_params("parallel", "parallel"),
        name="hg_rec",
    )(r3(q), r3(lf), r3(k), r3(v), r3(gate), gn)
    return o.reshape(T, D), s


def _hg_dec_kernel(q_ref, lf_ref, k_ref, v_ref, gate_ref, gn_ref, s0_ref, *rest, out_layer):
    o_ref, s_ref = rest[-2:]
    _, nb, nh, _, vd = s0_ref.shape
    for other in range(s_ref.shape[0]):
        if other != out_layer:
            s_ref[other] = jnp.zeros(s_ref.shape[1:], s_ref.dtype)
    for b in range(nb):
        outs = []
        for h in range(nh):
            f_col = jnp.exp(lf_ref[b, :, h:h + 1])
            k_col = k_ref[b, :, h:h + 1]
            q_col = q_ref[b, :, h:h + 1]
            v_row = v_ref[b, :, h * vd:(h + 1) * vd]
            s_new = f_col * s0_ref[0, b, h] + k_col * v_row
            s_ref[out_layer, b, h] = s_new
            o = jnp.sum(s_new * q_col, axis=0, keepdims=True)
            outs.append(_rms(o, gn_ref[:, h * vd:(h + 1) * vd]))
        o_all = jnp.concatenate(outs, axis=1) * gate_ref[b]
        o_ref[b] = o_all.astype(o_ref.dtype)


def _hg_dec(q, lf, k, v, gate, gn, state_all, layer, new_state_all):
    n_layers, nseq, nh, kd, vd = state_all.shape
    D = nh * vd
    nb = DEC_STATE_BLOCK if nseq % DEC_STATE_BLOCK == 0 else 1
    km = lambda a: jnp.transpose(a.reshape(nseq, nh, kd), (0, 2, 1))
    col = pl.BlockSpec((nb, kd, nh), lambda i: (i, 0, 0))
    rowv = pl.BlockSpec((nb, 1, D), lambda i: (i, 0, 0))
    st = pl.BlockSpec((1, nb, nh, kd, vd), lambda i: (layer, i, 0, 0, 0))
    in_specs = [col, col, col, rowv, rowv, pl.BlockSpec((1, D), lambda i: (0, 0)), st]
    args = [km(q), km(lf), km(k), v.reshape(nseq, 1, D), gate.reshape(nseq, 1, D), gn, state_all]
    if new_state_all is None:
        st_out, out_layer, aliases = pl.BlockSpec((n_layers, nb, nh, kd, vd),
                                                  lambda i: (0, i, 0, 0, 0)), layer, {}
    else:
        in_specs.append(pl.BlockSpec(memory_space=pl.ANY))
        args.append(new_state_all)
        st_out, out_layer, aliases = st, 0, {len(args) - 1: 1}
    o, s = pl.pallas_call(
        functools.partial(_hg_dec_kernel, out_layer=out_layer),
        grid=(nseq // nb,),
        in_specs=in_specs,
        out_specs=[rowv, st_out],
        out_shape=[jax.ShapeDtypeStruct((nseq, 1, D), BF16),
                   jax.ShapeDtypeStruct(state_all.shape, F32)],
        input_output_aliases=aliases,
        compiler_params=_params("parallel"),
        name="hg_dec",
    )(*args)
    return o.reshape(nseq, D), s


def _mlp_kernel(h_ref, mix_ref, wmix_ref, g_ref, wup_ref, wdn_ref, gf_ref, o_ref, xn_scr,
                *, final_norm):
    j = pl.program_id(1)

    @pl.when(j == 0)
    def _():
        h1 = h_ref[...] + _dot(mix_ref[...], wmix_ref[...])
        o_ref[...] = h1
        xn_scr[...] = _rms(h1, g_ref[...]).astype(BF16)

    u = jnp.maximum(_dot(xn_scr[...], wup_ref[...]), 0.0)
    o_ref[...] += _dot((u * u).astype(BF16), wdn_ref[...])

    if final_norm:
        @pl.when(j == pl.num_programs(1) - 1)
        def _():
            o_ref[...] = _rms(o_ref[...], gf_ref[...])


def _mlp(h, mix, wmix, g, wup, wdn, gf, tm, final_norm, keep=None):
    T, D = h.shape
    dff = wup.shape[1]
    tf = min(dff, MLP_FF_BLOCK)
    if keep is None:
        n_tiles = T // tm
        row_in = lambda: pl.BlockSpec((tm, D), lambda i, j: (i, 0))
    else:
        n_seq, seq_len, skip = keep
        tm = _row_tile(seq_len - skip, cap=512)
        per_seq = (seq_len - skip) // tm
        n_tiles = n_seq * per_seq
        row_in = lambda: pl.BlockSpec(
            (pl.Element(tm), pl.Element(D)),
            lambda i, j: (pl.multiple_of((i // per_seq) * seq_len + skip + (i % per_seq) * tm,
                                         math.gcd(seq_len, skip, tm)), 0))
    vec = lambda: pl.BlockSpec((1, D), lambda i, j: (0, 0))
    return pl.pallas_call(
        functools.partial(_mlp_kernel, final_norm=final_norm),
        grid=(n_tiles, dff // tf),
        in_specs=[row_in(), row_in(), pl.BlockSpec((D, D), lambda i, j: (0, 0)), vec(),
                  pl.BlockSpec((D, tf), lambda i, j: (0, j)),
                  pl.BlockSpec((tf, D), lambda i, j: (j, 0)), vec()],
        out_specs=pl.BlockSpec((tm, D), lambda i, j: (i, 0)),
        out_shape=jax.ShapeDtypeStruct((n_tiles * tm, D), F32),
        scratch_shapes=[pltpu.VMEM((tm, D), BF16)],
        compiler_params=_params("parallel", "arbitrary"),
        name="mlp",
    )(h, mix, wmix, g, wup, wdn, gf)


def _kv_kernel(h_ref, g_ref, wdkv_ref, gkv_ref, wkr_ref, wkrs_ref, cos_ref, sin_ref,
               ckv_ref, kr_ref, kvb_ref):
    hn = _rms(h_ref[...], g_ref[...]).astype(BF16)
    ckv = _rms(_dot(hn, wdkv_ref[...]), gkv_ref[...])
    kr = _dot(hn, wkr_ref[...]) * cos_ref[...] + _dot(hn, wkrs_ref[...]) * sin_ref[...]
    ckv_ref[...] = ckv
    kr_ref[...] = kr
    r, rd = ckv.shape[1], kr.shape[1]
    kvb_ref[:, :r] = ckv.astype(BF16)
    kvb_ref[:, r:r + rd] = kr.astype(BF16)
    kvb_ref[:, r + rd:] = jnp.zeros((kr.shape[0], kvb_ref.shape[1] - r - rd), BF16)


def _kv(h, g, wdkv, gkv, wkr, wkrs, cos, sin, tm):
    T, D = h.shape
    r, rd = wdkv.shape[1], wkr.shape[1]
    npos = cos.shape[0] // tm
    row = lambda n: pl.BlockSpec((tm, n), lambda i: (i, 0))
    vec = lambda n: pl.BlockSpec((1, n), lambda i: (0, 0))
    mat = lambda a: pl.BlockSpec(a.shape, lambda i: (0, 0))
    tab = pl.BlockSpec((tm, rd), lambda i: (i % npos, 0))
    return pl.pallas_call(
        _kv_kernel,
        grid=(T // tm,),
        in_specs=[row(D), vec(D), mat(wdkv), vec(r), mat(wkr), mat(wkrs), tab, tab],
        out_specs=[row(r), row(rd), row(r + ROPE_PAD)],
        out_shape=[jax.ShapeDtypeStruct((T, r), F32), jax.ShapeDtypeStruct((T, rd), F32),
                   jax.ShapeDtypeStruct((T, r + ROPE_PAD), BF16)],
        compiler_params=_params("parallel"),
        name="shared_kv",
    )(h, g, wdkv, gkv, wkr, wkrs, cos, sin)


def _q_kernel(h_ref, g_ref, wdq_ref, gq_ref, wn_ref, wr_ref, wuk_ref, cos_ref, sin_ref, o_ref,
              *, scale):
    xn = _rms(h_ref[...], g_ref[...]).astype(BF16)
    cq = _rms(_dot(xn, wdq_ref[...]), gq_ref[...]).astype(BF16)
    qn = (_dot(cq, wn_ref[...]) * scale).astype(BF16)
    nh, nd, r = wuk_ref.shape
    y = _dot(cq, wr_ref[...])
    rd = y.shape[1] // nh
    qr = ((y * cos_ref[...] + _swap_rope_halves(y, rd) * sin_ref[...]) * scale).astype(BF16)
    for h in range(nh):
        o_ref[h, :, :r] = _dot(qn[:, h * nd:(h + 1) * nd], wuk_ref[h]).astype(BF16)
        o_ref[h, :, r:r + rd] = qr[:, h * rd:(h + 1) * rd]
        o_ref[h, :, r + rd:] = jnp.zeros((qr.shape[0], o_ref.shape[2] - r - rd), BF16)


def _mla_q(h, g, wdq, gq, wn, wr, wuk, cos, sin, tm, scale):
    T, D = h.shape
    nh, _, r = wuk.shape
    npos = cos.shape[0] // tm
    vec = lambda n: pl.BlockSpec((1, n), lambda i: (0, 0))
    mat = lambda a: pl.BlockSpec(a.shape, lambda i: (0,) * a.ndim)
    tab = pl.BlockSpec((tm, cos.shape[1]), lambda i: (i % npos, 0))
    return pl.pallas_call(
        functools.partial(_q_kernel, scale=scale),
        grid=(T // tm,),
        in_specs=[pl.BlockSpec((tm, D), lambda i: (i, 0)), vec(D), mat(wdq), vec(wdq.shape[1]),
                  mat(wn), mat(wr), mat(wuk), tab, tab],
        out_specs=pl.BlockSpec((nh, tm, r + ROPE_PAD), lambda i: (0, i, 0)),
        out_shape=jax.ShapeDtypeStruct((nh, T, r + ROPE_PAD), BF16),
        compiler_params=_params("parallel"),
        name="mla_q",
    )(h, g, wdq, gq, wn, wr, wuk, cos, sin)


def _attn_kernel(q_ref, kv_ref, kvt_ref, wuv_ref, o_ref, m_scr, l_scr, acc_scr, *, seq_len, r):
    i = pl.program_id(1)
    nh = q_ref.shape[0]
    tq = ATTN_TILE
    n_full = seq_len // tq
    tail = seq_len - n_full * tq
    vd = wuv_ref.shape[1] // nh

    def run_full():
        m_scr[...] = jnp.full_like(m_scr, MASK_VALUE)
        l_scr[...] = jnp.zeros_like(l_scr)
        acc_scr[...] = jnp.zeros_like(acc_scr)

        def scores(h, j):
            kv = kv_ref[0, pl.ds(pl.multiple_of(j * tq, tq), tq), :]
            return lax.dot_general(kv, q_ref[h, 0], NT, preferred_element_type=F32)

        def head_block(h, j, s, s_next, mask):
            if s_next is not None:
                s_next[h] = scores(h, j + 1)
            yield
            if mask is not None:
                s = jnp.where(mask, s, MASK_VALUE)
            m = m_scr[h]
            m_new = jnp.maximum(m, jnp.max(s, axis=0, keepdims=True))
            alpha = jnp.exp2(m - m_new)
            p = jnp.exp2(s - m_new)
            l_scr[h] = alpha * l_scr[h] + jnp.sum(p, axis=0, keepdims=True)
            m_scr[h] = m_new
            kvt = kvt_ref[0, :, pl.ds(pl.multiple_of(j * tq, tq), tq)]
            pv = _dot(kvt, p.astype(BF16))
            yield
            acc_scr[h] = alpha * acc_scr[h] + pv

        def body(j, s_cur):
            s_next = [None] * nh
            _run_staggered([head_block(h, j, s_cur[h], s_next, None) for h in range(nh)])
            return tuple(s_next)

        s_diag = lax.fori_loop(0, i, body, tuple(scores(h, 0) for h in range(nh)))
        key_loc = lax.broadcasted_iota(jnp.int32, (tq, tq), 0)
        qry_loc = lax.broadcasted_iota(jnp.int32, (tq, tq), 1)
        _run_staggered([head_block(h, i, s_diag[h], None, key_loc <= qry_loc) for h in range(nh)])
        for h in range(nh):
            o_t = (acc_scr[h] * (1.0 / l_scr[h])).astype(BF16)
            o_ref[0, :, h * vd:(h + 1) * vd] = lax.dot_general(
                o_t, wuv_ref[:, h * vd:(h + 1) * vd], TN,
                preferred_element_type=F32).astype(o_ref.dtype)

    def run_tail(rows):
        m_rows = nh * rows
        q = q_ref[:, 0, :rows, :].reshape(m_rows, q_ref.shape[-1])
        kv = kv_ref[0]
        s = lax.dot_general(q, kv, NT, preferred_element_type=F32)
        t_loc = lax.broadcasted_iota(jnp.int32, s.shape, 0) % rows
        s = jnp.where(lax.broadcasted_iota(jnp.int32, s.shape, 1) <= n_full * tq + t_loc,
                      s, MASK_VALUE)
        p = jnp.exp2(s - jnp.max(s, axis=-1, keepdims=True))
        l = jnp.sum(p, axis=-1, keepdims=True)
        o_lat = (_dot(p.astype(BF16), kv[:, :r]) / l).astype(BF16)
        for h in range(nh):
            o_ref[0, :rows, h * vd:(h + 1) * vd] = _dot(
                o_lat[h * rows:(h + 1) * rows], wuv_ref[:, h * vd:(h + 1) * vd]).astype(o_ref.dtype)

    if n_full:
        pl.when(i < n_full)(run_full)
    if tail:
        pl.when(i == n_full)(lambda: run_tail(tail))


def _attn_prompt(qcat, kvb, wuv, batch, seq_len):
    nh, T, w = qcat.shape
    r = w - ROPE_PAD
    D = wuv.shape[1]
    tq = ATTN_TILE
    nq = pl.cdiv(seq_len, tq)
    kv3 = kvb.reshape(batch, seq_len, w)
    kvt = jnp.swapaxes(kv3[:, :, :r], 1, 2)
    out = pl.pallas_call(
        functools.partial(_attn_kernel, seq_len=seq_len, r=r),
        grid=(batch, nq),
        in_specs=[pl.BlockSpec((nh, 1, tq, w), lambda b, i: (0, b, i, 0)),
                  pl.BlockSpec((1, seq_len, w), lambda b, i: (b, 0, 0)),
                  pl.BlockSpec((1, r, seq_len), lambda b, i: (b, 0, 0)),
                  pl.BlockSpec(wuv.shape, lambda b, i: (0, 0))],
        out_specs=pl.BlockSpec((1, tq, D), lambda b, i: (b, i, 0)),
        out_shape=jax.ShapeDtypeStruct((batch, seq_len, D), BF16),
        scratch_shapes=[pltpu.VMEM((nh, 1, tq), F32), pltpu.VMEM((nh, 1, tq), F32),
                        pltpu.VMEM((nh, r, tq), F32)],
        compiler_params=_params("parallel", "parallel"),
        name="attn_prompt",
    )(qcat.reshape(nh, batch, seq_len, w), kv3, kvt, wuv)
    return out.reshape(T, D)


def _attn_dec_kernel(pt_ref, q_ref, kvn_ref, wuv_ref, ckv_hbm, krt_hbm, o_ref,
                     ckv_buf, krt_buf, kv_scr, krt_scr, sem, *, r, rd):
    b = pl.program_id(0)
    n_pg, _, page = krt_buf.shape[1:]

    def page_copies(seq, slot, p):
        idx = pt_ref[seq, p]
        rows = pl.ds(pl.multiple_of(p * page, page), page)
        return (pltpu.make_async_copy(ckv_hbm.at[idx], ckv_buf.at[slot, rows], sem.at[0, slot]),
                pltpu.make_async_copy(krt_hbm.at[idx], krt_buf.at[slot, p], sem.at[1, slot]))

    def for_pages(seq, slot, act):
        def body(p, carry):
            for c in page_copies(seq, slot, p):
                act(c)
            return carry
        lax.fori_loop(0, n_pg, body, 0)

    slot = b % 2

    @pl.when(b == 0)
    def _():
        for_pages(0, 0, lambda c: c.start())

    @pl.when(b + 1 < pl.num_programs(0))
    def _():
        for_pages(b + 1, 1 - slot, lambda c: c.start())

    for_pages(b, slot, lambda c: c.wait())

    q = q_ref[0]
    nh = q.shape[0]
    half = (n_pg // 2) * page
    s_parts = []
    for side in range(2):
        lanes = slice(side * r, (side + 1) * r)
        keys = slice(side * half, (side + 1) * half)
        kv_scr[:, lanes] = ckv_buf[slot, keys, :].astype(BF16)
        for p in range(side * (n_pg // 2), (side + 1) * (n_pg // 2)):
            krt_scr[:, p * page:(p + 1) * page] = krt_buf[slot, p].astype(BF16)
        s_parts.append(lax.dot_general(q[:, :r], kv_scr[:, lanes], NT, preferred_element_type=F32)
                       + _dot(q[:, r:r + rd], krt_scr[:, keys]))
    kvn = kvn_ref[0].astype(F32)
    s_new = jnp.sum(q.astype(F32) * kvn, axis=-1, keepdims=True)
    m = s_new
    for s in s_parts:
        m = jnp.maximum(m, jnp.max(s, axis=-1, keepdims=True))
    p_new = jnp.exp2(s_new - m)
    p_parts = [jnp.exp2(s - m) for s in s_parts]
    l = p_new + sum(jnp.sum(p, axis=-1, keepdims=True) for p in p_parts)
    pv = _dot(jnp.concatenate(p_parts, axis=0).astype(BF16), kv_scr[...])
    o_lat = (pv[:nh, :r] + pv[nh:, r:] + p_new * kvn[:, :r]) / l
    full = _dot(o_lat.astype(BF16), wuv_ref[...])
    vd = full.shape[1] // nh
    head_of_lane = lax.broadcasted_iota(jnp.int32, full.shape, 1) // vd
    head_of_row = lax.broadcasted_iota(jnp.int32, full.shape, 0)
    o_ref[0] = jnp.sum(jnp.where(head_of_lane == head_of_row, full, 0.0), axis=0,
                       keepdims=True).astype(o_ref.dtype)


def _attn_decode(qcat, kvb_new, cache_ckv, cache_krt, page_table, wuv):
    nh, nseq, w = qcat.shape
    r = w - ROPE_PAD
    rd, page = cache_krt.shape[1:]
    n_pg = page_table.shape[1]
    assert n_pg % 2 == 0
    D = wuv.shape[1]
    q = jnp.transpose(qcat, (1, 0, 2))
    grid_spec = pltpu.PrefetchScalarGridSpec(
        num_scalar_prefetch=1,
        grid=(nseq,),
        in_specs=[pl.BlockSpec((1, nh, w), lambda b, pt: (b, 0, 0)),
                  pl.BlockSpec((1, 1, w), lambda b, pt: (b, 0, 0)),
                  pl.BlockSpec(wuv.shape, lambda b, pt: (0, 0)),
                  pl.BlockSpec(memory_space=pl.ANY),
                  pl.BlockSpec(memory_space=pl.ANY)],
        out_specs=pl.BlockSpec((1, 1, D), lambda b, pt: (b, 0, 0)),
        scratch_shapes=[pltpu.VMEM((2, n_pg * page, r), F32), pltpu.VMEM((2, n_pg, rd, page), F32),
                        pltpu.VMEM((n_pg // 2 * page, 2 * r), BF16), pltpu.VMEM((rd, n_pg * page), BF16),
                        pltpu.SemaphoreType.DMA((2, 2))],
    )
    out = pl.pallas_call(
        functools.partial(_attn_dec_kernel, r=r, rd=rd),
        grid_spec=grid_spec,
        out_shape=jax.ShapeDtypeStruct((nseq, 1, D), BF16),
        compiler_params=_params("arbitrary"),
        name="attn_decode",
    )(page_table, q, kvb_new.reshape(nseq, 1, w), wuv, cache_ckv, cache_krt)
    return out.reshape(nseq, D)


def _rope_tables(pos, rope_dim):
    half = rope_dim // 2
    inv_freq = ROPE_THETA ** (-jnp.arange(half, dtype=F32) / half)
    ang = pos.astype(F32)[:, None] * inv_freq[None, :]
    cos, sin = jnp.cos(ang), jnp.sin(ang)
    return jnp.concatenate([cos, cos], axis=1), jnp.concatenate([-sin, sin], axis=1)


def _swap_halves(w, rope_dim):
    lead = w.shape[0]
    w3 = w.reshape(lead, -1, rope_dim)
    half = rope_dim // 2
    return jnp.concatenate([w3[..., half:], w3[..., :half]], axis=-1).reshape(w.shape)


def _prep_weights(p):
    bf = lambda a: a.astype(BF16)
    depth = p["norm_mix"].shape[0]
    n_a = p["hg_wq"].shape[0]
    r, nh, nd = p["w_uk"].shape
    rope_dim = p["w_kr"].shape[1]
    lbs = jax.nn.softmax(p["hg_lower_bounds"].astype(F32), axis=0)
    lbs = jnp.cumsum(lbs, axis=0) - lbs[0]
    w_uq = p["w_uq"]
    nb = w_uq.shape[0]
    wn = w_uq[..., :nd].reshape(nb, w_uq.shape[1], nh * nd)
    wr = w_uq[..., nd:].reshape(nb, w_uq.shape[1], nh * rope_dim)
    return dict(
        depth=depth, n_a=n_a, rope_dim=rope_dim, nope_dim=nd,
        norm_mix=p["norm_mix"][:, None, :], norm_mlp=p["norm_mlp"][:, None, :],
        norm_final=p["norm_final"][None, :],
        hg_wq=bf(p["hg_wq"]), hg_wf=bf(p["hg_wf"]), hg_wi=bf(p["hg_wi"]), hg_wg=bf(p["hg_wg"]),
        hg_wo=bf(p["hg_wo"]), hg_gnorm=p["hg_gnorm"][:, None, :], lbs=lbs[:, None, :],
        norm_kv=p["norm_kv"][None, :], w_dkv=bf(p["w_dkv"]), kv_norm=p["kv_norm"][None, :],
        w_kr=bf(p["w_kr"]), w_kr_sw=bf(_swap_halves(p["w_kr"], rope_dim)),
        w_ukt=bf(jnp.transpose(p["w_uk"], (1, 2, 0))),
        w_uv=bf(p["w_uv"].reshape(r, -1)),
        w_dq=bf(p["w_dq"]), q_norm=p["q_norm"][:, None, :],
        w_uq_n=bf(wn), w_uq_r=bf(wr),
        w_o=bf(p["w_o"]), w_up=bf(p["w_up"]), w_down=bf(p["w_down"]),
    )


def _trunk(h, pos_tile, tm, w, hg_rec_fn, attend_fn, act_dtype, out_rows=None):
    rope_dim = w["rope_dim"]
    cos, sin = _rope_tables(pos_tile, rope_dim)
    cos_q, sin_q = jnp.tile(cos, (1, MLA_HEADS)), jnp.tile(sin, (1, MLA_HEADS))
    scale = float((w["nope_dim"] + rope_dim) ** -0.5) * LOG2E
    depth, n_a = w["depth"], w["n_a"]
    states = []
    ckv = krope = kvb = None
    for layer in range(depth):
        last = layer == depth - 1
        if layer < n_a:
            q, lf, k, v, gate = _hg_proj(h, w["norm_mix"][layer], w["hg_wq"][layer], w["hg_wf"][layer],
                                         w["hg_wi"][layer], w["hg_wg"][layer], w["lbs"][layer],
                                         tm, act_dtype)
            mix, s_new = hg_rec_fn(layer, q, lf, k, v, gate, w["hg_gnorm"][layer])
            states.append(s_new)
            w_mix = w["hg_wo"][layer]
        else:
            j = layer - n_a
            qcat = _mla_q(h, w["norm_mix"][layer], w["w_dq"][j], w["q_norm"][j], w["w_uq_n"][j],
                          w["w_uq_r"][j], w["w_ukt"], cos_q, sin_q, tm, scale)
            mix = attend_fn(qcat, kvb)
            w_mix = w["w_o"][j]
        h = _mlp(h, mix, w_mix, w["norm_mlp"][layer], w["w_up"][layer], w["w_down"][layer],
                 w["norm_final"], tm, final_norm=last, keep=out_rows if last else None)
        if layer == n_a - 1:
            ckv, krope, kvb = _kv(h, w["norm_kv"], w["w_dkv"], w["kv_norm"], w["w_kr"], w["w_kr_sw"],
                                  cos, sin, tm)
    return h, states, ckv, krope


def kernel(x_prompt, x_sample, state_hgrn, cache_ckv, cache_krope, page_table, meta_tokens, norm_mix, norm_mlp, norm_final, hg_wq, hg_wf, hg_wi, hg_wg, hg_gnorm, hg_wo, hg_lower_bounds, norm_kv, w_dkv, kv_norm, w_kr, w_uk, w_uv, w_dq, q_norm, w_uq, w_o, w_up, w_down):
    w = _prep_weights(dict(
        norm_mix=norm_mix, norm_mlp=norm_mlp, norm_final=norm_final, hg_wq=hg_wq, hg_wf=hg_wf,
        hg_wi=hg_wi, hg_wg=hg_wg, hg_gnorm=hg_gnorm, hg_wo=hg_wo, hg_lower_bounds=hg_lower_bounds,
        norm_kv=norm_kv, w_dkv=w_dkv, kv_norm=kv_norm, w_kr=w_kr, w_uk=w_uk, w_uv=w_uv, w_dq=w_dq,
        q_norm=q_norm, w_uq=w_uq, w_o=w_o, w_up=w_up, w_down=w_down))

    b_s, dec, d = x_sample.shape
    assert dec == 1
    past_len = page_table.shape[1] * cache_ckv.shape[1]
    pos_s = jnp.full((b_s,), past_len, dtype=jnp.int32)
    cache_krt = jnp.transpose(cache_krope, (0, 2, 1))
    new_states = [None]

    def hg_dec_fn(layer, q, lf, k, v, gate, gn):
        mix, new_states[0] = _hg_dec(q, lf, k, v, gate, gn, state_hgrn, layer, new_states[0])
        return mix, new_states[0]

    out_s, st_s, ckv_s, kr_s = _trunk(
        x_sample.reshape(b_s, d), pos_s, b_s, w, hg_dec_fn,
        lambda qcat, kvb: _attn_decode(qcat, kvb, cache_ckv, cache_krt, page_table, w["w_uv"]),
        F32)
    b_p, seq, d = x_prompt.shape
    n_meta = meta_tokens.shape[0]
    seq_len = n_meta + seq
    meta = jnp.broadcast_to(meta_tokens.astype(x_prompt.dtype)[None], (b_p, n_meta, d))
    h_p = jnp.concatenate([meta, x_prompt], axis=1).reshape(b_p * seq_len, d)
    tm_p = _row_tile(seq_len)
    out_p, st_p, ckv_p, kr_p = _trunk(
        h_p, jnp.arange(seq_len, dtype=jnp.int32), tm_p, w,
        lambda layer, q, lf, k, v, gate, gn: _hg_rec(q, lf, k, v, gate, gn, b_p, seq_len),
        lambda qcat, kvb: _attn_prompt(qcat, kvb, w["w_uv"], b_p, seq_len),
        BF16, out_rows=(b_p, seq_len, n_meta))
    st_p = jnp.stack(st_p)
    y_prompt = out_p.reshape(b_p, seq, d)
    ckv_prompt = ckv_p.reshape(b_p, seq_len, -1)
    krope_prompt = kr_p.reshape(b_p, seq_len, -1)

    return (y_prompt, out_s.reshape(b_s, dec, d), st_p, ckv_prompt, krope_prompt,
            st_s[-1].astype(state_hgrn.dtype), ckv_s.reshape(b_s, dec, -1), kr_s.reshape(b_s, dec, -1))
```

```python
import functools
import math

import jax
import jax.numpy as jnp
from jax import lax
from jax.experimental import pallas as pl
from jax.experimental.pallas import tpu as pltpu

F32 = jnp.float32
BF16 = jnp.bfloat16

EPS = 1e-6
LB_FLOOR = 1e-30
ROPE_THETA = 10000.0
MASK_VALUE = -1e30

HG_HEADS = 8
MLA_HEADS = 8
HG_CHUNK = 64
HG_SUB = 16
HG_HEADS_PER_STEP = 4
HG_CHUNK_UNROLL = 4
SUBLANES = 8
LOG2E = 1.4426950408889634
ATTN_TILE = 256
ROPE_PAD = 128
DEC_STATE_BLOCK = 4
MLP_FF_BLOCK = 2048
PROJ_COL_BLOCK = 256
VMEM_LIMIT_BYTES = 56 * 1024 * 1024

NT = (((1,), (1,)), ((), ()))
TN = (((0,), (0,)), ((), ()))


def _params(*sem):
    return pltpu.CompilerParams(dimension_semantics=sem, vmem_limit_bytes=VMEM_LIMIT_BYTES)


def _rms(x, g):
    return x * lax.rsqrt(jnp.mean(x * x, axis=-1, keepdims=True) + EPS) * g


def _silu(x):
    h = 0.5 * x
    return h + h * jnp.tanh(h)


def _neg_abs(x):
    bits = lax.bitcast_convert_type(x, jnp.uint32) | jnp.uint32(0x80000000)
    return lax.bitcast_convert_type(bits, F32)


def _dot(a, b):
    return jnp.dot(a, b, preferred_element_type=F32)


def _swap_rope_halves(y, rope_dim):
    n = y.shape[1]
    half = rope_dim // 2
    lane = lax.broadcasted_iota(jnp.int32, y.shape, 1)
    return jnp.where(lane % rope_dim < half, pltpu.roll(y, n - half, 1), pltpu.roll(y, half, 1))


def _run_staggered(gens):
    started, nxt = [], 0
    while nxt < len(gens) or started:
        if nxt < len(gens):
            started.insert(0, gens[nxt])
            nxt += 1
        started = [g for g in started if next(g, True) is None]


def _row_tile(seq_len, cap=1024):
    best = None
    for t in range(16, min(seq_len, cap) + 1, 16):
        if seq_len % t == 0:
            best = t
    assert best is not None, seq_len
    return best


def _hg_proj_kernel(h_ref, g_ref, wq_ref, wf_ref, wi_ref, wg_ref, lb_ref,
                    q_ref, lf_ref, k_ref, v_ref, gate_ref, xn_scr):
    xn_scr[...] = _rms(h_ref[...], g_ref[...]).astype(BF16)
    lb = lb_ref[...]
    log_1m_lb = jnp.log1p(-lb)
    log_lb = jnp.log(jnp.maximum(lb, LB_FLOOR))

    def silu_to(ref):
        def store(cols, y):
            ref[:, cols] = _silu(y).astype(ref.dtype)
        return store

    def forget_gate(cols, z):
        t = jnp.exp2(_neg_abs(z * LOG2E))
        t1 = 1.0 + t
        a = log_1m_lb[:, cols] + (jnp.minimum(z, 0.0) - jnp.log(t1))
        u = jnp.exp2(_neg_abs((a - log_lb[:, cols]) * LOG2E))
        lf_ref[:, cols] = jnp.maximum(a, log_lb[:, cols]) + jnp.log(1.0 + u)
        k_ref[:, cols] = ((1.0 - lb[:, cols])
                          * (jnp.where(z >= 0.0, t, 1.0) / t1)).astype(k_ref.dtype)

    def value(cols, y):
        v_ref[:, cols] = y.astype(v_ref.dtype)

    def item(cols, w_ref, epilogue):
        y = _dot(xn_scr[...], w_ref[:, cols])
        yield
        epilogue(cols, y)

    n = wq_ref.shape[1]
    cb = min(n, PROJ_COL_BLOCK)
    _run_staggered([item(slice(c, c + cb), w_ref, ep)
                    for c in range(0, n, cb)
                    for w_ref, ep in ((wf_ref, forget_gate), (wq_ref, silu_to(q_ref)),
                                      (wi_ref, value), (wg_ref, silu_to(gate_ref)))])


def _hg_proj(h, g, wq, wf, wi, wg, lb, tm, act_dtype):
    T, D = h.shape
    N = wq.shape[1]
    row = pl.BlockSpec((tm, D), lambda i: (i, 0))
    vec = lambda n: pl.BlockSpec((1, n), lambda i: (0, 0))
    mat = lambda a: pl.BlockSpec(a.shape, lambda i: (0, 0))
    out = lambda n: pl.BlockSpec((tm, n), lambda i: (i, 0))
    return pl.pallas_call(
        _hg_proj_kernel,
        grid=(T // tm,),
        in_specs=[row, vec(D), mat(wq), mat(wf), mat(wi), mat(wg), vec(N)],
        out_specs=[out(N), out(N), out(N), out(D), out(D)],
        out_shape=[jax.ShapeDtypeStruct((T, N), act_dtype), jax.ShapeDtypeStruct((T, N), F32),
                   jax.ShapeDtypeStruct((T, N), act_dtype), jax.ShapeDtypeStruct((T, D), act_dtype),
                   jax.ShapeDtypeStruct((T, D), act_dtype)],
        scratch_shapes=[pltpu.VMEM((tm, D), BF16)],
        compiler_params=_params("parallel"),
        name="hg_proj",
    )(h, g, wq, wf, wi, wg, lb)


def _hg_rec_kernel(q_ref, lf_ref, k_ref, v_ref, gate_ref, gn_ref, o_ref, s_ref, st_scr, row_scr,
                   *, seq_len, heads):
    st_scr[...] = jnp.zeros_like(st_scr)
    kd = q_ref.shape[-1] // heads
    vd = v_ref.shape[-1] // heads
    ones_kk = jnp.ones((kd, kd), BF16)
    t_idx = lax.broadcasted_iota(jnp.int32, (SUBLANES, 1), 0)

    def head_chunk(g, r0, C, u=0):
        rows = pl.ds(r0, C)
        kl = slice(g * kd, (g + 1) * kd)
        vl = slice(g * vd, (g + 1) * vd)
        n_sub = C // HG_SUB
        lf = lf_ref[0, rows, kl] * LOG2E
        tri = (lax.broadcasted_iota(jnp.int32, (C, C), 0)
               >= lax.broadcasted_iota(jnp.int32, (C, C), 1)).astype(BF16)
        hi = lf.astype(BF16)
        r1 = lf - hi.astype(F32)
        mid = r1.astype(BF16)
        lo = (r1 - mid.astype(F32)).astype(BF16)
        b = _dot(tri, hi) + _dot(tri, mid) + _dot(tri, lo)
        q = q_ref[0, rows, kl].astype(F32)
        k = k_ref[0, rows, kl].astype(F32)
        vb = v_ref[0, rows, vl].astype(BF16)
        vf = vb.astype(F32)
        row_scr[u, g, 0, :C] = b
        row_scr[u, g, 1, :C] = k
        row_scr[u, g, 2, :C] = vf
        b_row = lambda t: row_scr[u, g, 0, t:t + 1, :]
        k_row = lambda t: row_scr[u, g, 1, t:t + 1, :]
        v_row = lambda t: row_scr[u, g, 2, t:t + 1, :]
        yield
        st = st_scr[g]
        o = lax.dot_general((q * jnp.exp2(b)).astype(BF16), st.astype(BF16), NT,
                            preferred_element_type=F32)
        scores = []
        for i in range(1, n_sub):
            lo_r = i * HG_SUB
            anchor = b[lo_r - 1:lo_r]
            qd = (q[lo_r:lo_r + HG_SUB] * jnp.exp2(b[lo_r:lo_r + HG_SUB] - anchor)).astype(BF16)
            kdec = (k[:lo_r] * jnp.exp2(anchor - b[:lo_r])).astype(BF16)
            scores.append(lax.dot_general(qd, kdec, NT, preferred_element_type=F32))
        b_last = b[C - 1:C]
        kdec = (k * jnp.exp2(b_last - b)).astype(BF16)
        st_scr[g] = st * jnp.exp2(b_last) + lax.dot_general(vb, kdec, TN,
                                                            preferred_element_type=F32)
        yield
        off = [None] + [_dot(sc.astype(BF16), vb[:(i + 1) * HG_SUB])
                        for i, sc in enumerate(scores)]
        sums = []
        for i in range(n_sub):
            lo_r, hi_r = i * HG_SUB, (i + 1) * HG_SUB
            b_i, q_i = b[lo_r:hi_r], q[lo_r:hi_r]
            prods = []
            for s in range(HG_SUB):
                for p0 in range((s // SUBLANES) * SUBLANES, HG_SUB, SUBLANES):
                    e = jnp.exp2(b_i[p0:p0 + SUBLANES] - b_row(lo_r + s))
                    if p0 <= s:
                        e = jnp.where(t_idx >= s - p0, e, 0.0)
                    prods.append(q_i[p0:p0 + SUBLANES] * e * k_row(lo_r + s))
            sums.append(_dot(jnp.concatenate(prods, axis=0).astype(BF16), ones_kk))
            yield
        parts = []
        for i in range(n_sub):
            lo_r = i * HG_SUB
            acc = {p0: (jnp.zeros((SUBLANES, vd), F32) if off[i] is None
                        else off[i][p0:p0 + SUBLANES]) for p0 in range(0, HG_SUB, SUBLANES)}
            n = 0
            for s in range(HG_SUB):
                for p0 in range((s // SUBLANES) * SUBLANES, HG_SUB, SUBLANES):
                    acc[p0] = acc[p0] + sums[i][n * SUBLANES:(n + 1) * SUBLANES] * v_row(lo_r + s)
                    n += 1
            parts.extend(acc[p0] for p0 in sorted(acc))
        o = o + jnp.concatenate(parts, axis=0)
        gate = gate_ref[0, rows, vl].astype(F32)
        o_ref[0, rows, vl] = (_rms(o, gn_ref[:, vl]) * gate).astype(o_ref.dtype)

    def chunks(starts, C):
        live = [head_chunk(g, r0, C, u) for u, r0 in enumerate(starts) for g in range(heads)]
        while live:
            live = [gen for gen in live if next(gen, True) is None]

    n_full = seq_len // HG_CHUNK
    tail = seq_len - n_full * HG_CHUNK
    unroll = HG_CHUNK_UNROLL if n_full % HG_CHUNK_UNROLL == 0 else 1

    def body(c, carry):
        base = pl.multiple_of(c * (unroll * HG_CHUNK), unroll * HG_CHUNK)
        chunks([base + u * HG_CHUNK for u in range(unroll)], HG_CHUNK)
        return carry

    lax.fori_loop(0, n_full // unroll, body, 0)
    if tail:
        chunks([n_full * HG_CHUNK], tail)
    for g in range(heads):
        s_ref[0, g] = st_scr[g].T


def _hg_rec(q, lf, k, v, gate, gn, batch, seq_len):
    T, N = q.shape
    D = v.shape[1]
    kd, vd = N // HG_HEADS, D // HG_HEADS
    G = HG_HEADS_PER_STEP
    assert seq_len % HG_SUB == 0 and HG_HEADS % G == 0
    r3 = lambda a: a.reshape(batch, seq_len, a.shape[1])
    seq_k = pl.BlockSpec((1, seq_len, G * kd), lambda b, h: (b, 0, h))
    seq_v = pl.BlockSpec((1, seq_len, G * vd), lambda b, h: (b, 0, h))
    o, s = pl.pallas_call(
        functools.partial(_hg_rec_kernel, seq_len=seq_len, heads=G),
        grid=(batch, HG_HEADS // G),
        in_specs=[seq_k, seq_k, seq_k, seq_v, seq_v,
                  pl.BlockSpec((1, G * vd), lambda b, h: (0, h))],
        out_specs=[seq_v, pl.BlockSpec((1, G, kd, vd), lambda b, h: (b, h, 0, 0))],
        out_shape=[jax.ShapeDtypeStruct((batch, seq_len, D), BF16),
                   jax.ShapeDtypeStruct((batch, HG_HEADS, kd, vd), F32)],
        scratch_shapes=[pltpu.VMEM((G, vd, kd), F32), pltpu.VMEM((HG_CHUNK_UNROLL, G, 3, HG_CHUNK, kd), F32)],
        compiler_params=_params("parallel", "parallel"),
        name="hg_rec",
    )(r3(q), r3(lf), r3(k), r3(v), r3(gate), gn)
    return o.reshape(T, D), s


def _hg_dec_kernel(q_ref, lf_ref, k_ref, v_ref, gate_ref, gn_ref, s0_ref, *rest, out_layer):
    o_ref, s_ref = rest[-2:]
    _, nb, nh, _, vd = s0_ref.shape
    for other in range(s_ref.shape[0]):
        if other != out_layer:
            s_ref[other] = jnp.zeros(s_ref.shape[1:], s_ref.dtype)
    for b in range(nb):
        outs = []
        for h in range(nh):
            f_col = jnp.exp(lf_ref[b, :, h:h + 1])
            k_col = k_ref[b, :, h:h + 1]
            q_col = q_ref[b, :, h:h + 1]
            v_row = v_ref[b, :, h * vd:(h + 1) * vd]
            s_new = f_col * s0_ref[0, b, h] + k_col * v_row
            s_ref[out_layer, b, h] = s_new
            o = jnp.sum(s_new * q_col, axis=0, keepdims=True)
            outs.append(_rms(o, gn_ref[:, h * vd:(h + 1) * vd]))
        o_all = jnp.concatenate(outs, axis=1) * gate_ref[b]
        o_ref[b] = o_all.astype(o_ref.dtype)


def _hg_dec(q, lf, k, v, gate, gn, state_all, layer, new_state_all):
    n_layers, nseq, nh, kd, vd = state_all.shape
    D = nh * vd
    nb = DEC_STATE_BLOCK if nseq % DEC_STATE_BLOCK == 0 else 1
    km = lambda a: jnp.transpose(a.reshape(nseq, nh, kd), (0, 2, 1))
    col = pl.BlockSpec((nb, kd, nh), lambda i: (i, 0, 0))
    rowv = pl.BlockSpec((nb, 1, D), lambda i: (i, 0, 0))
    st = pl.BlockSpec((1, nb, nh, kd, vd), lambda i: (layer, i, 0, 0, 0))
    in_specs = [col, col, col, rowv, rowv, pl.BlockSpec((1, D), lambda i: (0, 0)), st]
    args = [km(q), km(lf), km(k), v.reshape(nseq, 1, D), gate.reshape(nseq, 1, D), gn, state_all]
    if new_state_all is None:
        st_out, out_layer, aliases = pl.BlockSpec((n_layers, nb, nh, kd, vd),
                                                  lambda i: (0, i, 0, 0, 0)), layer, {}
    else:
        in_specs.append(pl.BlockSpec(memory_space=pl.ANY))
        args.append(new_state_all)
        st_out, out_layer, aliases = st, 0, {len(args) - 1: 1}
    o, s = pl.pallas_call(
        functools.partial(_hg_dec_kernel, out_layer=out_layer),
        grid=(nseq // nb,),
        in_specs=in_specs,
        out_specs=[rowv, st_out],
        out_shape=[jax.ShapeDtypeStruct((nseq, 1, D), BF16),
                   jax.ShapeDtypeStruct(state_all.shape, F32)],
        input_output_aliases=aliases,
        compiler_params=_params("parallel"),
        name="hg_dec",
    )(*args)
    return o.reshape(nseq, D), s


def _mlp_kernel(h_ref, mix_ref, wmix_ref, g_ref, wup_ref, wdn_ref, gf_ref, o_ref, xn_scr,
                *, final_norm):
    j = pl.program_id(1)

    @pl.when(j == 0)
    def _():
        h1 = h_ref[...] + _dot(mix_ref[...], wmix_ref[...])
        o_ref[...] = h1
        xn_scr[...] = _rms(h1, g_ref[...]).astype(BF16)

    u = jnp.maximum(_dot(xn_scr[...], wup_ref[...]), 0.0)
    o_ref[...] += _dot((u * u).astype(BF16), wdn_ref[...])

    if final_norm:
        @pl.when(j == pl.num_programs(1) - 1)
        def _():
            o_ref[...] = _rms(o_ref[...], gf_ref[...])


def _mlp(h, mix, wmix, g, wup, wdn, gf, tm, final_norm, keep=None):
    T, D = h.shape
    dff = wup.shape[1]
    tf = min(dff, MLP_FF_BLOCK)
    if keep is None:
        n_tiles = T // tm
        row_in = lambda: pl.BlockSpec((tm, D), lambda i, j: (i, 0))
    else:
        n_seq, seq_len, skip = keep
        tm = _row_tile(seq_len - skip, cap=512)
        per_seq = (seq_len - skip) // tm
        n_tiles = n_seq * per_seq
        row_in = lambda: pl.BlockSpec(
            (pl.Element(tm), pl.Element(D)),
            lambda i, j: (pl.multiple_of((i // per_seq) * seq_len + skip + (i % per_seq) * tm,
                                         math.gcd(seq_len, skip, tm)), 0))
    vec = lambda: pl.BlockSpec((1, D), lambda i, j: (0, 0))
    return pl.pallas_call(
        functools.partial(_mlp_kernel, final_norm=final_norm),
        grid=(n_tiles, dff // tf),
        in_specs=[row_in(), row_in(), pl.BlockSpec((D, D), lambda i, j: (0, 0)), vec(),
                  pl.BlockSpec((D, tf), lambda i, j: (0, j)),
                  pl.BlockSpec((tf, D), lambda i, j: (j, 0)), vec()],
        out_specs=pl.BlockSpec((tm, D), lambda i, j: (i, 0)),
        out_shape=jax.ShapeDtypeStruct((n_tiles * tm, D), F32),
        scratch_shapes=[pltpu.VMEM((tm, D), BF16)],
        compiler_params=_params("parallel", "arbitrary"),
        name="mlp",
    )(h, mix, wmix, g, wup, wdn, gf)


def _kv_kernel(h_ref, g_ref, wdkv_ref, gkv_ref, wkr_ref, wkrs_ref, cos_ref, sin_ref,
               ckv_ref, kr_ref, kvb_ref):
    hn = _rms(h_ref[...], g_ref[...]).astype(BF16)
    ckv = _rms(_dot(hn, wdkv_ref[...]), gkv_ref[...])
    kr = _dot(hn, wkr_ref[...]) * cos_ref[...] + _dot(hn, wkrs_ref[...]) * sin_ref[...]
    ckv_ref[...] = ckv
    kr_ref[...] = kr
    r, rd = ckv.shape[1], kr.shape[1]
    kvb_ref[:, :r] = ckv.astype(BF16)
    kvb_ref[:, r:r + rd] = kr.astype(BF16)
    kvb_ref[:, r + rd:] = jnp.zeros((kr.shape[0], kvb_ref.shape[1] - r - rd), BF16)


def _kv(h, g, wdkv, gkv, wkr, wkrs, cos, sin, tm):
    T, D = h.shape
    r, rd = wdkv.shape[1], wkr.shape[1]
    npos = cos.shape[0] // tm
    row = lambda n: pl.BlockSpec((tm, n), lambda i: (i, 0))
    vec = lambda n: pl.BlockSpec((1, n), lambda i: (0, 0))
    mat = lambda a: pl.BlockSpec(a.shape, lambda i: (0, 0))
    tab = pl.BlockSpec((tm, rd), lambda i: (i % npos, 0))
    return pl.pallas_call(
        _kv_kernel,
        grid=(T // tm,),
        in_specs=[row(D), vec(D), mat(wdkv), vec(r), mat(wkr), mat(wkrs), tab, tab],
        out_specs=[row(r), row(rd), row(r + ROPE_PAD)],
        out_shape=[jax.ShapeDtypeStruct((T, r), F32), jax.ShapeDtypeStruct((T, rd), F32),
                   jax.ShapeDtypeStruct((T, r + ROPE_PAD), BF16)],
        compiler_params=_params("parallel"),
        name="shared_kv",
    )(h, g, wdkv, gkv, wkr, wkrs, cos, sin)


def _q_kernel(h_ref, g_ref, wdq_ref, gq_ref, wn_ref, wr_ref, wuk_ref, cos_ref, sin_ref, o_ref,
              *, scale):
    xn = _rms(h_ref[...], g_ref[...]).astype(BF16)
    cq = _rms(_dot(xn, wdq_ref[...]), gq_ref[...]).astype(BF16)
    qn = (_dot(cq, wn_ref[...]) * scale).astype(BF16)
    nh, nd, r = wuk_ref.shape
    y = _dot(cq, wr_ref[...])
    rd = y.shape[1] // nh
    qr = ((y * cos_ref[...] + _swap_rope_halves(y, rd) * sin_ref[...]) * scale).astype(BF16)
    for h in range(nh):
        o_ref[h, :, :r] = _dot(qn[:, h * nd:(h + 1) * nd], wuk_ref[h]).astype(BF16)
        o_ref[h, :, r:r + rd] = qr[:, h * rd:(h + 1) * rd]
        o_ref[h, :, r + rd:] = jnp.zeros((qr.shape[0], o_ref.shape[2] - r - rd), BF16)


def _mla_q(h, g, wdq, gq, wn, wr, wuk, cos, sin, tm, scale):
    T, D = h.shape
    nh, _, r = wuk.shape
    npos = cos.shape[0] // tm
    vec = lambda n: pl.BlockSpec((1, n), lambda i: (0, 0))
    mat = lambda a: pl.BlockSpec(a.shape, lambda i: (0,) * a.ndim)
    tab = pl.BlockSpec((tm, cos.shape[1]), lambda i: (i % npos, 0))
    return pl.pallas_call(
        functools.partial(_q_kernel, scale=scale),
        grid=(T // tm,),
        in_specs=[pl.BlockSpec((tm, D), lambda i: (i, 0)), vec(D), mat(wdq), vec(wdq.shape[1]),
                  mat(wn), mat(wr), mat(wuk), tab, tab],
        out_specs=pl.BlockSpec((nh, tm, r + ROPE_PAD), lambda i: (0, i, 0)),
        out_shape=jax.ShapeDtypeStruct((nh, T, r + ROPE_PAD), BF16),
        compiler_params=_params("parallel"),
        name="mla_q",
    )(h, g, wdq, gq, wn, wr, wuk, cos, sin)


def _attn_kernel(q_ref, kv_ref, kvt_ref, wuv_ref, o_ref, *, seq_len, r):
    i = pl.program_id(1)
    nh = q_ref.shape[0]
    tq = ATTN_TILE
    n_full = seq_len // tq
    tail = seq_len - n_full * tq
    vd = wuv_ref.shape[1] // nh

    def run_full(n_blocks):
        nk = n_blocks * tq
        key_loc = lax.broadcasted_iota(jnp.int32, (tq, tq), 0)
        qry_loc = lax.broadcasted_iota(jnp.int32, (tq, tq), 1)

        def head(h):
            s = lax.dot_general(kv_ref[0, :nk, :], q_ref[h, 0], NT,
                                preferred_element_type=F32)
            yield
            s_diag = jnp.where(key_loc <= qry_loc, s[nk - tq:], MASK_VALUE)
            m = jnp.max(s_diag, axis=0, keepdims=True)
            if n_blocks > 1:
                m = jnp.maximum(m, jnp.max(s[:nk - tq], axis=0, keepdims=True))
            p = jnp.exp2(s_diag - m)
            if n_blocks > 1:
                p = jnp.concatenate([jnp.exp2(s[:nk - tq] - m), p], axis=0)
            l = jnp.sum(p, axis=0, keepdims=True)
            pv = _dot(kvt_ref[0, :, :nk], p.astype(BF16))
            yield
            o_t = (pv * (1.0 / l)).astype(BF16)
            o_ref[0, :, h * vd:(h + 1) * vd] = lax.dot_general(
                o_t, wuv_ref[:, h * vd:(h + 1) * vd], TN,
                preferred_element_type=F32).astype(o_ref.dtype)

        _run_staggered([head(h) for h in range(nh)])

    def run_tail(rows):
        m_rows = nh * rows
        q = q_ref[:, 0, :rows, :].reshape(m_rows, q_ref.shape[-1])
        kv = kv_ref[0]
        s = lax.dot_general(q, kv, NT, preferred_element_type=F32)
        t_loc = lax.broadcasted_iota(jnp.int32, s.shape, 0) % rows
        s = jnp.where(lax.broadcasted_iota(jnp.int32, s.shape, 1) <= n_full * tq + t_loc,
                      s, MASK_VALUE)
        p = jnp.exp2(s - jnp.max(s, axis=-1, keepdims=True))
        l = jnp.sum(p, axis=-1, keepdims=True)
        o_lat = (_dot(p.astype(BF16), kv[:, :r]) / l).astype(BF16)
        for h in range(nh):
            o_ref[0, :rows, h * vd:(h + 1) * vd] = _dot(
                o_lat[h * rows:(h + 1) * rows], wuv_ref[:, h * vd:(h + 1) * vd]).astype(o_ref.dtype)

    for c in range(n_full):
        pl.when(i == c)(functools.partial(run_full, c + 1))
    if tail:
        pl.when(i == n_full)(lambda: run_tail(tail))


def _attn_prompt(qcat, kvb, wuv, batch, seq_len):
    nh, T, w = qcat.shape
    r = w - ROPE_PAD
    D = wuv.shape[1]
    tq = ATTN_TILE
    nq = pl.cdiv(seq_len, tq)
    kv3 = kvb.reshape(batch, seq_len, w)
    kvt = jnp.swapaxes(kv3[:, :, :r], 1, 2)
    out = pl.pallas_call(
        functools.partial(_attn_kernel, seq_len=seq_len, r=r),
        grid=(batch, nq),
        in_specs=[pl.BlockSpec((nh, 1, tq, w), lambda b, i: (0, b, i, 0)),
                  pl.BlockSpec((1, seq_len, w), lambda b, i: (b, 0, 0)),
                  pl.BlockSpec((1, r, seq_len), lambda b, i: (b, 0, 0)),
                  pl.BlockSpec(wuv.shape, lambda b, i: (0, 0))],
        out_specs=pl.BlockSpec((1, tq, D), lambda b, i: (b, i, 0)),
        out_shape=jax.ShapeDtypeStruct((batch, seq_len, D), BF16),
        compiler_params=_params("parallel", "parallel"),
        name="attn_prompt",
    )(qcat.reshape(nh, batch, seq_len, w), kv3, kvt, wuv)
    return out.reshape(T, D)


def _attn_dec_kernel(pt_ref, q_ref, kvn_ref, wuv_ref, ckv_hbm, krt_hbm, o_ref,
                     ckv_buf, krt_buf, kv_scr, krt_scr, sem, *, r, rd):
    b = pl.program_id(0)
    n_pg, _, page = krt_buf.shape[1:]

    def page_copies(seq, slot, p):
        idx = pt_ref[seq, p]
        rows = pl.ds(pl.multiple_of(p * page, page), page)
        return (pltpu.make_async_copy(ckv_hbm.at[idx], ckv_buf.at[slot, rows], sem.at[0, slot]),
                pltpu.make_async_copy(krt_hbm.at[idx], krt_buf.at[slot, p], sem.at[1, slot]))

    def for_pages(seq, slot, act):
        def body(p, carry):
            for c in page_copies(seq, slot, p):
                act(c)
            return carry
        lax.fori_loop(0, n_pg, body, 0)

    slot = b % 2

    @pl.when(b == 0)
    def _():
        for_pages(0, 0, lambda c: c.start())

    @pl.when(b + 1 < pl.num_programs(0))
    def _():
        for_pages(b + 1, 1 - slot, lambda c: c.start())

    for_pages(b, slot, lambda c: c.wait())

    q = q_ref[0]
    nh = q.shape[0]
    half = (n_pg // 2) * page
    s_parts = []
    for side in range(2):
        lanes = slice(side * r, (side + 1) * r)
        keys = slice(side * half, (side + 1) * half)
        kv_scr[:, lanes] = ckv_buf[slot, keys, :].astype(BF16)
        for p in range(side * (n_pg // 2), (side + 1) * (n_pg // 2)):
            krt_scr[:, p * page:(p + 1) * page] = krt_buf[slot, p].astype(BF16)
        s_parts.append(lax.dot_general(q[:, :r], kv_scr[:, lanes], NT, preferred_element_type=F32)
                       + _dot(q[:, r:r + rd], krt_scr[:, keys]))
    kvn = kvn_ref[0].astype(F32)
    s_new = jnp.sum(q.astype(F32) * kvn, axis=-1, keepdims=True)
    m = s_new
    for s in s_parts:
        m = jnp.maximum(m, jnp.max(s, axis=-1, keepdims=True))
    p_new = jnp.exp2(s_new - m)
    p_parts = [jnp.exp2(s - m) for s in s_parts]
    l = p_new + sum(jnp.sum(p, axis=-1, keepdims=True) for p in p_parts)
    pv = _dot(jnp.concatenate(p_parts, axis=0).astype(BF16), kv_scr[...])
    o_lat = (pv[:nh, :r] + pv[nh:, r:] + p_new * kvn[:, :r]) / l
    full = _dot(o_lat.astype(BF16), wuv_ref[...])
    vd = full.shape[1] // nh
    head_of_lane = lax.broadcasted_iota(jnp.int32, full.shape, 1) // vd
    head_of_row = lax.broadcasted_iota(jnp.int32, full.shape, 0)
    o_ref[0] = jnp.sum(jnp.where(head_of_lane == head_of_row, full, 0.0), axis=0,
                       keepdims=True).astype(o_ref.dtype)


def _attn_decode(qcat, kvb_new, cache_ckv, cache_krt, page_table, wuv):
    nh, nseq, w = qcat.shape
    r = w - ROPE_PAD
    rd, page = cache_krt.shape[1:]
    n_pg = page_table.shape[1]
    assert n_pg % 2 == 0
    D = wuv.shape[1]
    q = jnp.transpose(qcat, (1, 0, 2))
    grid_spec = pltpu.PrefetchScalarGridSpec(
        num_scalar_prefetch=1,
        grid=(nseq,),
        in_specs=[pl.BlockSpec((1, nh, w), lambda b, pt: (b, 0, 0)),
                  pl.BlockSpec((1, 1, w), lambda b, pt: (b, 0, 0)),
                  pl.BlockSpec(wuv.shape, lambda b, pt: (0, 0)),
                  pl.BlockSpec(memory_space=pl.ANY),
                  pl.BlockSpec(memory_space=pl.ANY)],
        out_specs=pl.BlockSpec((1, 1, D), lambda b, pt: (b, 0, 0)),
        scratch_shapes=[pltpu.VMEM((2, n_pg * page, r), F32), pltpu.VMEM((2, n_pg, rd, page), F32),
                        pltpu.VMEM((n_pg // 2 * page, 2 * r), BF16), pltpu.VMEM((rd, n_pg * page), BF16),
                        pltpu.SemaphoreType.DMA((2, 2))],
    )
    out = pl.pallas_call(
        functools.partial(_attn_dec_kernel, r=r, rd=rd),
        grid_spec=grid_spec,
        out_shape=jax.ShapeDtypeStruct((nseq, 1, D), BF16),
        compiler_params=_params("arbitrary"),
        name="attn_decode",
    )(page_table, q, kvb_new.reshape(nseq, 1, w), wuv, cache_ckv, cache_krt)
    return out.reshape(nseq, D)


def _rope_tables(pos, rope_dim):
    half = rope_dim // 2
    inv_freq = ROPE_THETA ** (-jnp.arange(half, dtype=F32) / half)
    ang = pos.astype(F32)[:, None] * inv_freq[None, :]
    cos, sin = jnp.cos(ang), jnp.sin(ang)
    return jnp.concatenate([cos, cos], axis=1), jnp.concatenate([-sin, sin], axis=1)


def _swap_halves(w, rope_dim):
    lead = w.shape[0]
    w3 = w.reshape(lead, -1, rope_dim)
    half = rope_dim // 2
    return jnp.concatenate([w3[..., half:], w3[..., :half]], axis=-1).reshape(w.shape)


def _prep_weights(p):
    bf = lambda a: a.astype(BF16)
    depth = p["norm_mix"].shape[0]
    n_a = p["hg_wq"].shape[0]
    r, nh, nd = p["w_uk"].shape
    rope_dim = p["w_kr"].shape[1]
    lbs = jax.nn.softmax(p["hg_lower_bounds"].astype(F32), axis=0)
    lbs = jnp.cumsum(lbs, axis=0) - lbs[0]
    w_uq = p["w_uq"]
    nb = w_uq.shape[0]
    wn = w_uq[..., :nd].reshape(nb, w_uq.shape[1], nh * nd)
    wr = w_uq[..., nd:].reshape(nb, w_uq.shape[1], nh * rope_dim)
    return dict(
        depth=depth, n_a=n_a, rope_dim=rope_dim, nope_dim=nd,
        norm_mix=p["norm_mix"][:, None, :], norm_mlp=p["norm_mlp"][:, None, :],
        norm_final=p["norm_final"][None, :],
        hg_wq=bf(p["hg_wq"]), hg_wf=bf(p["hg_wf"]), hg_wi=bf(p["hg_wi"]), hg_wg=bf(p["hg_wg"]),
        hg_wo=bf(p["hg_wo"]), hg_gnorm=p["hg_gnorm"][:, None, :], lbs=lbs[:, None, :],
        norm_kv=p["norm_kv"][None, :], w_dkv=bf(p["w_dkv"]), kv_norm=p["kv_norm"][None, :],
        w_kr=bf(p["w_kr"]), w_kr_sw=bf(_swap_halves(p["w_kr"], rope_dim)),
        w_ukt=bf(jnp.transpose(p["w_uk"], (1, 2, 0))),
        w_uv=bf(p["w_uv"].reshape(r, -1)),
        w_dq=bf(p["w_dq"]), q_norm=p["q_norm"][:, None, :],
        w_uq_n=bf(wn), w_uq_r=bf(wr),
        w_o=bf(p["w_o"]), w_up=bf(p["w_up"]), w_down=bf(p["w_down"]),
    )


def _trunk(h, pos_tile, tm, w, hg_rec_fn, attend_fn, act_dtype, out_rows=None):
    rope_dim = w["rope_dim"]
    cos, sin = _rope_tables(pos_tile, rope_dim)
    cos_q, sin_q = jnp.tile(cos, (1, MLA_HEADS)), jnp.tile(sin, (1, MLA_HEADS))
    scale = float((w["nope_dim"] + rope_dim) ** -0.5) * LOG2E
    depth, n_a = w["depth"], w["n_a"]
    states = []
    ckv = krope = kvb = None
    for layer in range(depth):
        last = layer == depth - 1
        if layer < n_a:
            q, lf, k, v, gate = _hg_proj(h, w["norm_mix"][layer], w["hg_wq"][layer], w["hg_wf"][layer],
                                         w["hg_wi"][layer], w["hg_wg"][layer], w["lbs"][layer],
                                         tm, act_dtype)
            mix, s_new = hg_rec_fn(layer, q, lf, k, v, gate, w["hg_gnorm"][layer])
            states.append(s_new)
            w_mix = w["hg_wo"][layer]
        else:
            j = layer - n_a
            qcat = _mla_q(h, w["norm_mix"][layer], w["w_dq"][j], w["q_norm"][j], w["w_uq_n"][j],
                          w["w_uq_r"][j], w["w_ukt"], cos_q, sin_q, tm, scale)
            mix = attend_fn(qcat, kvb)
            w_mix = w["w_o"][j]
        h = _mlp(h, mix, w_mix, w["norm_mlp"][layer], w["w_up"][layer], w["w_down"][layer],
                 w["norm_final"], tm, final_norm=last, keep=out_rows if last else None)
        if layer == n_a - 1:
            ckv, krope, kvb = _kv(h, w["norm_kv"], w["w_dkv"], w["kv_norm"], w["w_kr"], w["w_kr_sw"],
                                  cos, sin, tm)
    return h, states, ckv, krope


def kernel(x_prompt, x_sample, state_hgrn, cache_ckv, cache_krope, page_table, meta_tokens, norm_mix, norm_mlp, norm_final, hg_wq, hg_wf, hg_wi, hg_wg, hg_gnorm, hg_wo, hg_lower_bounds, norm_kv, w_dkv, kv_norm, w_kr, w_uk, w_uv, w_dq, q_norm, w_uq, w_o, w_up, w_down):
    w = _prep_weights(dict(
        norm_mix=norm_mix, norm_mlp=norm_mlp, norm_final=norm_final, hg_wq=hg_wq, hg_wf=hg_wf,
        hg_wi=hg_wi, hg_wg=hg_wg, hg_gnorm=hg_gnorm, hg_wo=hg_wo, hg_lower_bounds=hg_lower_bounds,
        norm_kv=norm_kv, w_dkv=w_dkv, kv_norm=kv_norm, w_kr=w_kr, w_uk=w_uk, w_uv=w_uv, w_dq=w_dq,
        q_norm=q_norm, w_uq=w_uq, w_o=w_o, w_up=w_up, w_down=w_down))

    b_s, dec, d = x_sample.shape
    assert dec == 1
    past_len = page_table.shape[1] * cache_ckv.shape[1]
    pos_s = jnp.full((b_s,), past_len, dtype=jnp.int32)
    cache_krt = jnp.transpose(cache_krope, (0, 2, 1))
    new_states = [None]

    def hg_dec_fn(layer, q, lf, k, v, gate, gn):
        mix, new_states[0] = _hg_dec(q, lf, k, v, gate, gn, state_hgrn, layer, new_states[0])
        return mix, new_states[0]

    out_s, st_s, ckv_s, kr_s = _trunk(
        x_sample.reshape(b_s, d), pos_s, b_s, w, hg_dec_fn,
        lambda qcat, kvb: _attn_decode(qcat, kvb, cache_ckv, cache_krt, page_table, w["w_uv"]),
        F32)
    b_p, seq, d = x_prompt.shape
    n_meta = meta_tokens.shape[0]
    seq_len = n_meta + seq
    meta = jnp.broadcast_to(meta_tokens.astype(x_prompt.dtype)[None], (b_p, n_meta, d))
    h_p = jnp.concatenate([meta, x_prompt], axis=1).reshape(b_p * seq_len, d)
    tm_p = _row_tile(seq_len)
    out_p, st_p, ckv_p, kr_p = _trunk(
        h_p, jnp.arange(seq_len, dtype=jnp.int32), tm_p, w,
        lambda layer, q, lf, k, v, gate, gn: _hg_rec(q, lf, k, v, gate, gn, b_p, seq_len),
        lambda qcat, kvb: _attn_prompt(qcat, kvb, w["w_uv"], b_p, seq_len),
        BF16, out_rows=(b_p, seq_len, n_meta))
    st_p = jnp.stack(st_p)
    y_prompt = out_p.reshape(b_p, seq, d)
    ckv_prompt = ckv_p.reshape(b_p, seq_len, -1)
    krope_prompt = kr_p.reshape(b_p, seq_len, -1)

    return (y_prompt, out_s.reshape(b_s, dec, d), st_p, ckv_prompt, krope_prompt,
            st_s[-1].astype(state_hgrn.dtype), ckv_s.reshape(b_s, dec, -1), kr_s.reshape(b_s, dec, -1))
```
